```python
import jax
import jax.numpy as jnp
from jax import lax
import numpy as np

D_MODEL = 1024
BATCH = 4
SEQ = 8192
DEPTH = 1

ATT_GROUPS = ((128, 1), (512, 4), (2048, 16))
ATT_HEADS_PER_GROUP = 4
ATT_HEADS = ATT_HEADS_PER_GROUP * len(ATT_GROUPS)
ATT_HEAD_DIM = 128
ATT_BLOCK = 128
ATT_W = ATT_HEADS * ATT_HEAD_DIM
ATT_OUT_W = ATT_HEADS_PER_GROUP * ATT_HEAD_DIM
RET_HEADS = 4
RET_QK_DIM = D_MODEL // RET_HEADS
RET_V_DIM = 2 * D_MODEL // RET_HEADS
RET_QK_W = RET_HEADS * RET_QK_DIM
RET_V_W = RET_HEADS * RET_V_DIM
RET_CHUNK = 128
D_FF = 4 * D_MODEL
IN_SPLITS = (ATT_W, ATT_W, ATT_W, RET_QK_W, RET_QK_W, RET_V_W, RET_V_W, D_MODEL, D_MODEL)
IN_W = sum(IN_SPLITS)
EPS = 1e-6

kernel_name = 'hybrid_dilated_attention_retention_block'


def rmsnorm(x, g):
    xf = x.astype(jnp.float32)
    y = xf * lax.rsqrt(jnp.mean(xf * xf, axis=-1, keepdims=True) + EPS)
    return (y * g.astype(jnp.float32)).astype(x.dtype)


def alibi_slopes(n_heads):
    return jnp.asarray(2.0 ** (-8.0 * np.arange(1, n_heads + 1, dtype=np.float32) / n_heads), dtype=jnp.float32)


def dilated_group(q, k, v, slopes, window, dilation):
    B, S, H, Dh = q.shape
    span = dilation * ATT_BLOCK
    s_pad = -(-S // span) * span
    L = s_pad // dilation
    nb = L // ATT_BLOCK
    reach = window // dilation

    def to_blocks(t):
        t = jnp.pad(t, ((0, 0), (0, s_pad - S), (0, 0), (0, 0)))
        t = t.reshape(B, L, dilation, H, Dh).transpose(0, 2, 3, 1, 4)
        return t.reshape(B, dilation, H, nb, ATT_BLOCK, Dh)

    def with_prev(t):
        prev = jnp.pad(t, ((0, 0), (0, 0), (0, 0), (1, 0), (0, 0), (0, 0)))[:, :, :, :-1]
        return jnp.concatenate([prev, t], axis=4)

    qb = to_blocks(q)
    kb = with_prev(to_blocks(k))
    vb = with_prev(to_blocks(v))
    s = jnp.einsum('brhnqd,brhnkd->brhnqk', qb, kb, preferred_element_type=jnp.float32) * (Dh ** -0.5)
    qi = jnp.arange(ATT_BLOCK)[:, None]
    kj = jnp.arange(2 * ATT_BLOCK)[None, :]
    dist = ATT_BLOCK + qi - kj
    blk = jnp.arange(nb)[:, None, None]
    valid = (dist >= 0) & (dist <= reach) & ((blk - 1) * ATT_BLOCK + kj >= 0)
    bias = -slopes[:, None, None] * (dist * dilation).astype(jnp.float32)
    s = jnp.where(valid[None, None, None], s + bias[None, None, :, None], -jnp.inf)
    m = jnp.max(s, axis=-1)
    p = jnp.exp(s - m[..., None])
    den = jnp.sum(p, axis=-1)
    o = jnp.einsum('brhnqk,brhnkd->brhnqd', p, vb.astype(jnp.float32)) / den[..., None]
    lse = m + jnp.log(den)
    o = o.reshape(B, dilation, H, L, Dh).transpose(0, 3, 1, 2, 4).reshape(B, s_pad, H, Dh)[:, :S]
    lse = lse.reshape(B, dilation, H, L).transpose(0, 3, 1, 2).reshape(B, s_pad, H)[:, :S]
    return o, lse


def retention(q, k, v):
    B, S, H, dk = q.shape
    dv = v.shape[-1]
    C = RET_CHUNK
    N = S // C
    log_g = jnp.log(1.0 - 2.0 ** (-5.0 - jnp.arange(H, dtype=jnp.float32)))
    idx = jnp.arange(C, dtype=jnp.float32)
    diff = idx[:, None] - idx[None, :]
    decay = jnp.where(diff >= 0, jnp.exp(log_g[:, None, None] * jnp.maximum(diff, 0.0)), 0.0)
    xi = jnp.exp(log_g[None, :] * (idx[:, None] + 1.0))
    zeta = jnp.exp(log_g[None, :] * (C - 1.0 - idx[:, None]))
    g_chunk = jnp.exp(log_g * C)
    qc = q.astype(jnp.float32).reshape(B, N, C, H, dk)
    kc = (k.astype(jnp.float32) * (dk ** -0.5)).reshape(B, N, C, H, dk)
    vc = v.astype(jnp.float32).reshape(B, N, C, H, dv)
    s = jnp.einsum('bnqhd,bnkhd->bnhqk', qc, kc) * decay[None, None]
    inner = jnp.einsum('bnhqk,bnkhv->bnqhv', s, vc)
    kz = kc * zeta[None, None, :, :, None]

    def step(state, xs):
        q_i, kz_i, v_i = xs
        cross = jnp.einsum('bqhd,bhdv->bqhv', q_i, state)
        state = state * g_chunk[None, :, None, None] + jnp.einsum('bkhd,bkhv->bhdv', kz_i, v_i)
        return state, cross

    state0 = jnp.zeros((B, H, dk, dv), jnp.float32)
    _, cross = lax.scan(step, state0, (qc.transpose(1, 0, 2, 3, 4), kz.transpose(1, 0, 2, 3, 4), vc.transpose(1, 0, 2, 3, 4)))
    cross = cross.transpose(1, 0, 2, 3, 4) * xi[None, None, :, :, None]
    return (inner + cross).reshape(B, S, H, dv)


def head_groupnorm(o, g, b):
    B, S, H, dv = o.shape
    mu = jnp.mean(o, axis=-1, keepdims=True)
    var = jnp.mean(jnp.square(o - mu), axis=-1, keepdims=True)
    y = ((o - mu) * lax.rsqrt(var + EPS)).reshape(B, S, H * dv)
    return y * g.astype(jnp.float32) + b.astype(jnp.float32)


def setup_inputs(seed: int = 0) -> dict:
    key = jax.random.key(seed)
    ks = jax.random.split(key, 14)

    def nrm(k, shape, scale):
        return scale * jax.random.normal(k, shape, jnp.float32)

    return {
        'x': nrm(ks[0], (BATCH, SEQ, D_MODEL), 1.0),
        'norm1_g': 1.0 + nrm(ks[1], (DEPTH, D_MODEL), 0.02),
        'w_in': nrm(ks[2], (DEPTH, D_MODEL, IN_W), D_MODEL ** -0.5),
        'q_norm_g': 1.0 + nrm(ks[3], (DEPTH, ATT_HEADS, ATT_HEAD_DIM), 0.02),
        'k_norm_g': 1.0 + nrm(ks[4], (DEPTH, ATT_HEADS, ATT_HEAD_DIM), 0.02),
        'ret_gn_g': 1.0 + nrm(ks[5], (DEPTH, RET_V_W), 0.02),
        'ret_gn_b': nrm(ks[6], (DEPTH, RET_V_W), 0.02),
        'w_proj_a': nrm(ks[7], (DEPTH, ATT_OUT_W, D_MODEL), ATT_OUT_W ** -0.5),
        'w_proj_b': nrm(ks[8], (DEPTH, RET_V_W, D_MODEL), RET_V_W ** -0.5),
        'w_out': nrm(ks[9], (DEPTH, D_MODEL, D_MODEL), D_MODEL ** -0.5),
        'norm2_g': 1.0 + nrm(ks[10], (DEPTH, D_MODEL), 0.02),
        'w_up': nrm(ks[11], (DEPTH, D_MODEL, D_FF), D_MODEL ** -0.5),
        'w_down': nrm(ks[12], (DEPTH, D_FF, D_MODEL), D_FF ** -0.5),
    }


def reference(x, norm1_g, w_in, q_norm_g, k_norm_g, ret_gn_g, ret_gn_b, w_proj_a, w_proj_b, w_out, norm2_g, w_up, w_down):
    B, S, _ = x.shape
    slopes = alibi_slopes(ATT_HEADS)
    bounds = np.cumsum((0,) + IN_SPLITS).tolist()
    for l in range(DEPTH):
        xn = rmsnorm(x, norm1_g[l])
        wl = w_in[l]
        qa, ka, va, qr, kr, vr, gr, gate_a, gate_b = [xn @ wl[:, bounds[i]:bounds[i + 1]] for i in range(len(IN_SPLITS))]
        qa = rmsnorm(qa.reshape(B, S, ATT_HEADS, ATT_HEAD_DIM), q_norm_g[l])
        ka = rmsnorm(ka.reshape(B, S, ATT_HEADS, ATT_HEAD_DIM), k_norm_g[l])
        va = va.reshape(B, S, ATT_HEADS, ATT_HEAD_DIM)
        outs, lses = [], []
        for gi, (window, dilation) in enumerate(ATT_GROUPS):
            hs = slice(gi * ATT_HEADS_PER_GROUP, (gi + 1) * ATT_HEADS_PER_GROUP)
            o, lse = dilated_group(qa[:, :, hs], ka[:, :, hs], va[:, :, hs], slopes[hs], window, dilation)
            outs.append(o)
            lses.append(lse)
        alpha = jax.nn.softmax(jnp.stack(lses, axis=0), axis=0)
        o_a = jnp.sum(alpha[..., None] * jnp.stack(outs, axis=0), axis=0).reshape(B, S, ATT_OUT_W).astype(x.dtype)
        o_r = retention(qr.reshape(B, S, RET_HEADS, RET_QK_DIM), kr.reshape(B, S, RET_HEADS, RET_QK_DIM),
                        vr.reshape(B, S, RET_HEADS, RET_V_DIM))
        o_r = (head_groupnorm(o_r, ret_gn_g[l], ret_gn_b[l]) * jax.nn.silu(gr.astype(jnp.float32))).astype(x.dtype)
        y = jax.nn.sigmoid(gate_a) * (o_a @ w_proj_a[l]) + jax.nn.sigmoid(gate_b) * (o_r @ w_proj_b[l])
        x = x + y @ w_out[l]
        xn2 = rmsnorm(x, norm2_g[l])
        x = x + jnp.square(jax.nn.relu(xn2 @ w_up[l])) @ w_down[l]
    return x
```

```python
import functools

import numpy as np
import jax
import jax.numpy as jnp
from jax import lax
from jax.experimental import pallas as pl
from jax.experimental.pallas import tpu as pltpu

F32 = jnp.float32
BF16 = jnp.bfloat16

D_MODEL = 1024
ATT_GROUPS = ((128, 1), (512, 4), (2048, 16))
HPG = 4
ATT_HEADS = 12
DH = 128
ATT_BLOCK = 128
ATT_W = ATT_HEADS * DH
GRP_W = HPG * DH
RET_HEADS = 4
RET_DK = 256
RET_DV = 512
RET_QK_W = 1024
RET_V_W = 2048
D_FF = 4096
EPS = 1e-6
NEG = -1e30

VMEM_LIMIT = 56 * 1024 * 1024


def _cparams(sem):
    return pltpu.CompilerParams(dimension_semantics=sem, vmem_limit_bytes=VMEM_LIMIT)


def _const_spec(shape):
    nd = len(shape)
    return pl.BlockSpec(shape, lambda *_: (0,) * nd, pipeline_mode=pl.Buffered(1))


def _nt_dot(a, b):
    return lax.dot_general(a, b, (((1,), (1,)), ((), ())), preferred_element_type=F32)


def _tn_dot(a, b):
    return lax.dot_general(a, b, (((0,), (0,)), ((), ())), preferred_element_type=F32)


def _rms_rows(x, g):
    return x * lax.rsqrt(jnp.mean(x * x, axis=-1, keepdims=True) + EPS) * g


def _proj_att_kernel(x_ref, g1_ref, w_ref, hg_ref, o_ref, xn_ref):
    j = pl.program_id(1)

    @pl.when(j == 0)
    def _():
        xn_ref[...] = _rms_rows(x_ref[...], g1_ref[...]).astype(BF16)

    acc = jnp.dot(xn_ref[...], w_ref[...], preferred_element_type=F32)

    @pl.when(j < 2)
    def _():
        for h in range(ATT_HEADS):
            hs = slice(h * DH, (h + 1) * DH)
            t = acc[:, hs]
            r = lax.rsqrt(jnp.mean(t * t, axis=-1, keepdims=True) + EPS)
            o_ref[:, hs] = (t * r * hg_ref[0, :, hs]).astype(BF16)

    @pl.when(j == 2)
    def _():
        o_ref[...] = acc.astype(BF16)


def _proj_att(x2, g1, w_att, hg, tm):
    T = x2.shape[0]
    return pl.pallas_call(
        _proj_att_kernel,
        grid=(T // tm, 3),
        in_specs=[
            pl.BlockSpec((tm, D_MODEL), lambda i, j: (i, 0)),
            _const_spec((1, D_MODEL)),
            pl.BlockSpec((D_MODEL, ATT_W), lambda i, j: (0, j)),
            pl.BlockSpec((1, 1, ATT_W), lambda i, j: (jnp.minimum(j, 1), 0, 0)),
        ],
        out_specs=pl.BlockSpec((tm, ATT_W), lambda i, j: (i, j)),
        out_shape=jax.ShapeDtypeStruct((T, 3 * ATT_W), BF16),
        scratch_shapes=[pltpu.VMEM((tm, D_MODEL), BF16)],
        compiler_params=_cparams(("parallel", "arbitrary")),
        name="proj_att",
    )(x2, g1, w_att, hg)


def _proj_rest_kernel(x_ref, g1_ref, w_ref, o_ref, xn_ref):
    @pl.when(pl.program_id(1) == 0)
    def _():
        xn_ref[...] = _rms_rows(x_ref[...], g1_ref[...]).astype(BF16)

    o_ref[...] = jnp.dot(xn_ref[...], w_ref[...], preferred_element_type=F32).astype(BF16)


def _proj_rest(x2, g1, w_rest, tm, tn):
    T = x2.shape[0]
    n = w_rest.shape[1]
    return pl.pallas_call(
        _proj_rest_kernel,
        grid=(T // tm, n // tn),
        in_specs=[
            pl.BlockSpec((tm, D_MODEL), lambda i, j: (i, 0)),
            _const_spec((1, D_MODEL)),
            pl.BlockSpec((D_MODEL, tn), lambda i, j: (0, j)),
        ],
        out_specs=pl.BlockSpec((tm, tn), lambda i, j: (i, j)),
        out_shape=jax.ShapeDtypeStruct((T, n), BF16),
        scratch_shapes=[pltpu.VMEM((tm, D_MODEL), BF16)],
        compiler_params=_cparams(("parallel", "arbitrary")),
        name="proj_rest",
    )(x2, g1, w_rest)


def _attn_kernel(q_ref, kc_ref, vc_ref, kp_ref, vp_ref, bias_ref, bias0_ref, o_ref, lse_ref, *, tq):
    first = pl.program_id(2) == 0
    lane = lax.broadcasted_iota(jnp.int32, (ATT_BLOCK, DH), 1)
    for jb in range(tq // ATT_BLOCK):
        rows = slice(jb * ATT_BLOCK, (jb + 1) * ATT_BLOCK)
        lse_blk = None
        for h in range(HPG):
            hs = slice(h * DH, (h + 1) * DH)
            q = q_ref[rows, hs]
            if jb == 0:
                k = jnp.concatenate([kp_ref[:, hs], kc_ref[0:ATT_BLOCK, hs]], axis=0)
                v = jnp.concatenate([vp_ref[:, hs], vc_ref[0:ATT_BLOCK, hs]], axis=0)
                b = jnp.where(first, bias0_ref[h], bias_ref[h])
            else:
                kr = slice((jb - 1) * ATT_BLOCK, (jb + 1) * ATT_BLOCK)
                k = kc_ref[kr, hs]
                v = vc_ref[kr, hs]
                b = bias_ref[h]
            s = _nt_dot(q, k) + b
            m = jnp.max(s, axis=-1, keepdims=True)
            p = jnp.exp(s - m)
            den = jnp.sum(p, axis=-1, keepdims=True)
            o = jnp.dot(p.astype(BF16), v, preferred_element_type=F32) * (1.0 / den)
            o_ref[rows, hs] = o.astype(BF16)
            lse = jnp.broadcast_to(m + jnp.log(den), (ATT_BLOCK, DH))
            lse_blk = lse if h == 0 else jnp.where(lane >= h * (DH // HPG), lse, lse_blk)
        lse_ref[rows, :] = lse_blk


def _attention(a_mat, gi, batch, seq, tq):
    window, dil = ATT_GROUPS[gi]
    T = a_mat.shape[0]
    L = seq // dil
    nq = L // tq
    ncb = 3 * ATT_W // GRP_W
    sub = tq // ATT_BLOCK
    a_view = a_mat.reshape(T // dil, dil * 3 * ATT_W)

    slopes = 2.0 ** (-8.0 * np.arange(1, ATT_HEADS + 1, dtype=np.float32) / ATT_HEADS)
    slopes = slopes[gi * HPG:(gi + 1) * HPG]
    qi = np.arange(ATT_BLOCK)[:, None]
    kj = np.arange(2 * ATT_BLOCK)[None, :]
    dist = ATT_BLOCK + qi - kj
    valid = (dist >= 0) & (dist <= window // dil)
    bias = np.where(valid[None], -slopes[:, None, None] * (dist * dil).astype(np.float32)[None], NEG).astype(np.float32)
    bias0 = np.where((kj >= ATT_BLOCK)[None], bias, NEG).astype(np.float32)

    def cur(col):
        return pl.BlockSpec((tq, GRP_W), lambda b, r, i: (b * nq + i, r * ncb + col))

    def prev(col):
        return pl.BlockSpec((ATT_BLOCK, GRP_W),
                            lambda b, r, i: (b * nq * sub + jnp.maximum(i * sub - 1, 0), r * ncb + col))

    o, lse = pl.pallas_call(
        functools.partial(_attn_kernel, tq=tq),
        grid=(batch, dil, nq),
        in_specs=[cur(gi), cur(3 + gi), cur(6 + gi), prev(3 + gi), prev(6 + gi),
                  _const_spec(bias.shape), _const_spec(bias0.shape)],
        out_specs=[pl.BlockSpec((tq, GRP_W), lambda b, r, i: (b * nq + i, r)),
                   pl.BlockSpec((tq, DH), lambda b, r, i: (b * nq + i, r))],
        out_shape=[jax.ShapeDtypeStruct((T // dil, dil * GRP_W), BF16),
                   jax.ShapeDtypeStruct((T // dil, dil * DH), F32)],
        compiler_params=_cparams(("parallel", "parallel", "parallel")),
        name=f"attn_d{dil}",
    )(a_view, a_view, a_view, a_view, a_view, jnp.asarray(bias), jnp.asarray(bias0))
    return o.reshape(T, GRP_W), lse.reshape(T, DH)


def _ret_kernel(q_ref, k_ref, v_ref, g_ref, gng_ref, gnb_ref, dec_ref, xi_ref, zeta_ref, gch_ref,
                o_ref, state_ref, *, chunk, nchunk):
    @pl.when(pl.program_id(2) == 0)
    def _():
        state_ref[...] = jnp.zeros_like(state_ref)

    for c in range(nchunk):
        rows = slice(c * chunk, (c + 1) * chunk)
        q = q_ref[rows, :]
        k = k_ref[rows, :]
        v = v_ref[rows, :]
        s = _nt_dot(q, k) * dec_ref[0]
        inner = jnp.dot(s.astype(BF16), v, preferred_element_type=F32)
        st = state_ref[...]
        cross = jnp.dot(q, st.astype(BF16), preferred_element_type=F32) * xi_ref[0]
        kz = (k.astype(F32) * zeta_ref[0]).astype(BF16)
        state_ref[...] = st * gch_ref[0] + _tn_dot(kz, v)
        o = inner + cross
        mu = jnp.mean(o, axis=-1, keepdims=True)
        oc = o - mu
        var = jnp.mean(oc * oc, axis=-1, keepdims=True)
        y = oc * lax.rsqrt(var + EPS) * gng_ref[...] + gnb_ref[...]
        gate = g_ref[rows, :].astype(F32)
        o_ref[rows, :] = (y * gate * (1.0 / (1.0 + jnp.exp(-gate)))).astype(BF16)


def _retention(r_mat, gn_g, gn_b, batch, seq, ts, chunk):
    T = r_mat.shape[0]
    ns = seq // ts
    H = RET_HEADS
    log_g = np.log(1.0 - 2.0 ** (-5.0 - np.arange(H, dtype=np.float64)))
    idx = np.arange(chunk, dtype=np.float64)
    diff = idx[:, None] - idx[None, :]
    scale = RET_DK ** -0.5
    dec = np.where(diff >= 0, np.exp(log_g[:, None, None] * np.maximum(diff, 0.0)), 0.0) * scale
    xi = np.exp(log_g[:, None] * (idx[None, :] + 1.0))
    zeta = np.exp(log_g[:, None] * (chunk - 1.0 - idx[None, :])) * scale
    gch = np.exp(log_g * chunk)
    dec = jnp.asarray(dec, F32)
    xi = jnp.asarray(np.broadcast_to(xi[:, :, None], (H, chunk, RET_DV)), F32)
    zeta = jnp.asarray(np.broadcast_to(zeta[:, :, None], (H, chunk, RET_DK)), F32)
    gch = jnp.asarray(np.broadcast_to(gch[:, None, None], (H, 1, RET_DV)), F32)

    return pl.pallas_call(
        functools.partial(_ret_kernel, chunk=chunk, nchunk=ts // chunk),
        grid=(batch, H, ns),
        in_specs=[
            pl.BlockSpec((ts, RET_DK), lambda b, h, i: (b * ns + i, h)),
            pl.BlockSpec((ts, RET_DK), lambda b, h, i: (b * ns + i, RET_QK_W // RET_DK + h)),
            pl.BlockSpec((ts, RET_DV), lambda b, h, i: (b * ns + i, 2 * RET_QK_W // RET_DV + h)),
            pl.BlockSpec((ts, RET_DV), lambda b, h, i: (b * ns + i, (2 * RET_QK_W + RET_V_W) // RET_DV + h)),
            pl.BlockSpec((1, RET_DV), lambda b, h, i: (0, h)),
            pl.BlockSpec((1, RET_DV), lambda b, h, i: (0, h)),
            pl.BlockSpec((1, chunk, chunk), lambda b, h, i: (h, 0, 0)),
            pl.BlockSpec((1, chunk, RET_DV), lambda b, h, i: (h, 0, 0)),
            pl.BlockSpec((1, chunk, RET_DK), lambda b, h, i: (h, 0, 0)),
            pl.BlockSpec((1, 1, RET_DV), lambda b, h, i: (h, 0, 0)),
        ],
        out_specs=pl.BlockSpec((ts, RET_DV), lambda b, h, i: (b * ns + i, h)),
        out_shape=jax.ShapeDtypeStruct((T, RET_V_W), BF16),
        scratch_shapes=[pltpu.VMEM((RET_DK, RET_DV), F32)],
        compiler_params=_cparams(("parallel", "parallel", "arbitrary")),
        name="retention",
    )(r_mat, r_mat, r_mat, r_mat, gn_g, gn_b, dec, xi, zeta, gch)


def _post_kernel(x_ref, o1_ref, o2_ref, o3_ref, l1_ref, l2_ref, l3_ref, or_ref, ga_ref, gb_ref,
                 wa_ref, wb_ref, wo_ref, g2_ref, wu_ref, wd_ref, out_ref, oa_ref, *, ff_chunk):
    l1, l2, l3 = l1_ref[...], l2_ref[...], l3_ref[...]
    lm = jnp.maximum(jnp.maximum(l1, l2), l3)
    e1, e2, e3 = jnp.exp(l1 - lm), jnp.exp(l2 - lm), jnp.exp(l3 - lm)
    inv = 1.0 / (e1 + e2 + e3)
    a1, a2, a3 = e1 * inv, e2 * inv, e3 * inv
    for h in range(HPG):
        hs = slice(h * DH, (h + 1) * DH)
        c = h * (DH // HPG)
        oa = (a1[:, c:c + 1] * o1_ref[:, hs].astype(F32)
              + a2[:, c:c + 1] * o2_ref[:, hs].astype(F32)
              + a3[:, c:c + 1] * o3_ref[:, hs].astype(F32))
        oa_ref[:, hs] = oa.astype(BF16)

    ya = jnp.dot(oa_ref[...], wa_ref[...], preferred_element_type=F32)
    yb = jnp.dot(or_ref[...], wb_ref[...], preferred_element_type=F32)
    sa = 1.0 / (1.0 + jnp.exp(-ga_ref[...].astype(F32)))
    sb = 1.0 / (1.0 + jnp.exp(-gb_ref[...].astype(F32)))
    y = (sa * ya + sb * yb).astype(BF16)
    x1 = x_ref[...] + jnp.dot(y, wo_ref[...], preferred_element_type=F32)
    xn2 = _rms_rows(x1, g2_ref[...]).astype(BF16)
    acc = x1
    for c in range(D_FF // ff_chunk):
        cs = slice(c * ff_chunk, (c + 1) * ff_chunk)
        hcol = jnp.maximum(jnp.dot(xn2, wu_ref[:, cs], preferred_element_type=F32), 0.0)
        acc = acc + jnp.dot((hcol * hcol).astype(BF16), wd_ref[cs, :], preferred_element_type=F32)
    out_ref[...] = acc


def _post(x2, o_att, lse_att, o_r, r_mat, wa, wb, wo, g2, wu, wd, tm, ff_chunk):
    T = x2.shape[0]
    gate_cb = (2 * RET_QK_W + 2 * RET_V_W) // D_MODEL

    def row(w):
        return pl.BlockSpec((tm, w), lambda i: (i, 0))

    return pl.pallas_call(
        functools.partial(_post_kernel, ff_chunk=ff_chunk),
        grid=(T // tm,),
        in_specs=[row(D_MODEL), row(GRP_W), row(GRP_W), row(GRP_W), row(DH), row(DH), row(DH), row(RET_V_W),
                  pl.BlockSpec((tm, D_MODEL), lambda i: (i, gate_cb)),
                  pl.BlockSpec((tm, D_MODEL), lambda i: (i, gate_cb + 1)),
                  _const_spec(wa.shape), _const_spec(wb.shape), _const_spec(wo.shape), _const_spec((1, D_MODEL)),
                  _const_spec(wu.shape), _const_spec(wd.shape)],
        out_specs=row(D_MODEL),
        out_shape=jax.ShapeDtypeStruct((T, D_MODEL), F32),
        scratch_shapes=[pltpu.VMEM((tm, GRP_W), BF16)],
        compiler_params=_cparams(("parallel",)),
        name="post",
    )(x2, *o_att, *lse_att, o_r, r_mat, r_mat, wa, wb, wo, g2, wu, wd)


def kernel(x, norm1_g, w_in, q_norm_g, k_norm_g, ret_gn_g, ret_gn_b, w_proj_a, w_proj_b, w_out, norm2_g, w_up, w_down):
    B, S, D = x.shape
    T = B * S
    depth = w_in.shape[0]
    x2 = x.reshape(T, D)
    for l in range(depth):
        w_bf = w_in[l].astype(BF16)
        w_att = w_bf[:, :3 * ATT_W]
        w_rest = w_bf[:, 3 * ATT_W:]
        hg = jnp.stack([q_norm_g[l].reshape(1, ATT_W) * (DH ** -0.5), k_norm_g[l].reshape(1, ATT_W)], axis=0)
        g1 = norm1_g[l].reshape(1, D)

        a_mat = _proj_att(x2, g1, w_att, hg, tm=1024)
        r_mat = _proj_rest(x2, g1, w_rest, tm=1024, tn=2048)

        o_att, lse_att = [], []
        for gi in range(len(ATT_GROUPS)):
            o, lse = _attention(a_mat, gi, B, S, tq=512)
            o_att.append(o)
            lse_att.append(lse)

        o_r = _retention(r_mat, ret_gn_g[l].reshape(1, RET_V_W), ret_gn_b[l].reshape(1, RET_V_W), B, S, ts=1024, chunk=256)

        x2 = _post(x2, o_att, lse_att, o_r, r_mat,
                   w_proj_a[l].astype(BF16), w_proj_b[l].astype(BF16), w_out[l].astype(BF16),
                   norm2_g[l].reshape(1, D), w_up[l].astype(BF16), w_down[l].astype(BF16),
                   tm=512, ff_chunk=1024)
    return x2.reshape(B, S, D)
```

```python
import functools

import numpy as np
import jax
import jax.numpy as jnp
from jax import lax
from jax.experimental import pallas as pl
from jax.experimental.pallas import tpu as pltpu

F32 = jnp.float32
BF16 = jnp.bfloat16

D_MODEL = 1024
ATT_GROUPS = ((128, 1), (512, 4), (2048, 16))
HPG = 4
ATT_HEADS = 12
DH = 128
LANE = 128
NSLAB = D_MODEL // LANE
ATT_BLOCK = 128
ATT_W = ATT_HEADS * DH
GRP_W = HPG * DH
RET_HEADS = 4
RET_DK = 256
RET_DV = 512
RET_QK_W = 1024
RET_V_W = 2048
D_FF = 4096
EPS = 1e-6
NEG = -1e30

VMEM_LIMIT = 56 * 1024 * 1024


def _cparams(sem):
    return pltpu.CompilerParams(dimension_semantics=sem, vmem_limit_bytes=VMEM_LIMIT)


def _const_spec(shape):
    nd = len(shape)
    return pl.BlockSpec(shape, lambda *_: (0,) * nd, pipeline_mode=pl.Buffered(1))


def _nt_dot(a, b):
    return lax.dot_general(a, b, (((1,), (1,)), ((), ())), preferred_element_type=F32)


def _tn_dot(a, b):
    return lax.dot_general(a, b, (((0,), (0,)), ((), ())), preferred_element_type=F32)


def _rms_rows(x, g):
    return x * lax.rsqrt(jnp.mean(x * x, axis=-1, keepdims=True) + EPS) * g


def _proj_att_kernel(*refs, dil, tm, rc):
    x_refs = refs[:NSLAB]
    g1_ref, w_ref, hg_ref, o_ref = refs[NSLAB:]
    n = tm // dil
    for kc in range(tm // rc):
        segs, p = [], kc * rc
        while p < (kc + 1) * rc:
            r, l0 = divmod(p, n)
            ln = min(n - l0, (kc + 1) * rc - p)
            segs.append((r, l0, ln))
            p += ln
        pieces = []
        for (r, l0, ln) in segs:
            if dil == 1:
                xs = [x_refs[c][l0:l0 + ln, :] for c in range(NSLAB)]
            else:
                xs = [x_refs[c][pl.ds(r + l0 * dil, ln, stride=dil), :] for c in range(NSLAB)]
            ss = xs[0] * xs[0]
            for c in range(1, NSLAB):
                ss = ss + xs[c] * xs[c]
            scale = lax.rsqrt(jnp.sum(ss, axis=-1, keepdims=True) * (1.0 / D_MODEL) + EPS)
            pieces.append(jnp.concatenate(
                [(xs[c] * scale * g1_ref[:, c * LANE:(c + 1) * LANE]).astype(BF16) for c in range(NSLAB)], axis=1))
        xn = pieces[0] if len(pieces) == 1 else jnp.concatenate(pieces, axis=0)
        acc = jnp.dot(xn, w_ref[...], preferred_element_type=F32)
        outs = []
        for h in range(2 * HPG):
            hs = slice(h * DH, (h + 1) * DH)
            t = acc[:, hs]
            rr = lax.rsqrt(jnp.sum(t * t, axis=-1, keepdims=True) * (1.0 / DH) + EPS)
            outs.append((t * rr * hg_ref[:, hs]).astype(BF16))
        outs.append(acc[:, 2 * GRP_W:].astype(BF16))
        res = jnp.concatenate(outs, axis=1)
        off = 0
        for (r, l0, ln) in segs:
            o_ref[0, r, l0:l0 + ln, :] = res[off:off + ln]
            off += ln


def _proj_att(x2, g1, w_g, hg, dil, batch, seq, tm, rc):
    nt = seq // tm
    slab = [pl.BlockSpec((tm, LANE), functools.partial(lambda b, i, c: (b * nt + i, c), c=c)) for c in range(NSLAB)]
    return pl.pallas_call(
        functools.partial(_proj_att_kernel, dil=dil, tm=tm, rc=rc),
        grid=(batch, nt),
        in_specs=slab + [_const_spec((1, D_MODEL)), _const_spec(w_g.shape), _const_spec(hg.shape)],
        out_specs=pl.BlockSpec((1, dil, tm // dil, 3 * GRP_W), lambda b, i: (b, 0, i, 0)),
        out_shape=jax.ShapeDtypeStruct((batch, dil, seq // dil, 3 * GRP_W), BF16),
        compiler_params=_cparams(("parallel", "parallel")),
        name=f"proj_att_d{dil}",
    )(*([x2] * NSLAB), g1, w_g, hg)


def _proj_rest_kernel(x_ref, g1_ref, w_ref, o_ref, xn_ref):
    @pl.when(pl.program_id(1) == 0)
    def _():
        xn_ref[...] = _rms_rows(x_ref[...], g1_ref[...]).astype(BF16)

    o_ref[...] = jnp.dot(xn_ref[...], w_ref[...], preferred_element_type=F32).astype(BF16)


def _proj_rest(x2, g1, w_rest, tm, tn):
    T = x2.shape[0]
    n = w_rest.shape[1]
    return pl.pallas_call(
        _proj_rest_kernel,
        grid=(T // tm, n // tn),
        in_specs=[
            pl.BlockSpec((tm, D_MODEL), lambda i, j: (i, 0)),
            _const_spec((1, D_MODEL)),
            pl.BlockSpec((D_MODEL, tn), lambda i, j: (0, j)),
        ],
        out_specs=pl.BlockSpec((tm, tn), lambda i, j: (i, j)),
        out_shape=jax.ShapeDtypeStruct((T, n), BF16),
        scratch_shapes=[pltpu.VMEM((tm, D_MODEL), BF16)],
        compiler_params=_cparams(("parallel", "arbitrary")),
        name="proj_rest",
    )(x2, g1, w_rest)


def _attn_kernel(q_ref, kc_ref, vc_ref, kp_ref, vp_ref, bias_ref, bias0_ref, o_ref, lse_ref, *, tq):
    first = pl.program_id(2) == 0
    lane = lax.broadcasted_iota(jnp.int32, (ATT_BLOCK, DH), 1)
    for jb in range(tq // ATT_BLOCK):
        rows = slice(jb * ATT_BLOCK, (jb + 1) * ATT_BLOCK)
        lse_blk = None
        for h in range(HPG):
            hs = slice(h * DH, (h + 1) * DH)
            q = q_ref[0, 0, rows, hs]
            if jb == 0:
                k = jnp.concatenate([kp_ref[0, 0, :, hs], kc_ref[0, 0, 0:ATT_BLOCK, hs]], axis=0)
                v = jnp.concatenate([vp_ref[0, 0, :, hs], vc_ref[0, 0, 0:ATT_BLOCK, hs]], axis=0)
                b = jnp.where(first, bias0_ref[h], bias_ref[h])
            else:
                kr = slice((jb - 1) * ATT_BLOCK, (jb + 1) * ATT_BLOCK)
                k = kc_ref[0, 0, kr, hs]
                v = vc_ref[0, 0, kr, hs]
                b = bias_ref[h]
            s = _nt_dot(q, k) + b
            m = jnp.max(s, axis=-1, keepdims=True)
            p = jnp.exp(s - m)
            den = jnp.sum(p, axis=-1, keepdims=True)
            o = jnp.dot(p.astype(BF16), v, preferred_element_type=F32) * (1.0 / den)
            o_ref[0, 0, rows, hs] = o.astype(BF16)
            lse = jnp.broadcast_to(m + jnp.log(den), (ATT_BLOCK, DH))
            lse_blk = lse if h == 0 else jnp.where(lane >= h * (DH // HPG), lse, lse_blk)
        lse_ref[0, 0, rows, :] = lse_blk


def _attention(qkv, gi, tq):
    window, dil = ATT_GROUPS[gi]
    batch, _, L, _ = qkv.shape
    sub = tq // ATT_BLOCK

    slopes = 2.0 ** (-8.0 * np.arange(1, ATT_HEADS + 1, dtype=np.float32) / ATT_HEADS)
    slopes = slopes[gi * HPG:(gi + 1) * HPG]
    qi = np.arange(ATT_BLOCK)[:, None]
    kj = np.arange(2 * ATT_BLOCK)[None, :]
    dist = ATT_BLOCK + qi - kj
    valid = (dist >= 0) & (dist <= window // dil)
    bias = np.where(valid[None], -slopes[:, None, None] * (dist * dil).astype(np.float32)[None], NEG).astype(np.float32)
    bias0 = np.where((kj >= ATT_BLOCK)[None], bias, NEG).astype(np.float32)

    def cur(part):
        return pl.BlockSpec((1, 1, tq, GRP_W), lambda b, r, i: (b, r, i, part))

    def prev(part):
        return pl.BlockSpec((1, 1, ATT_BLOCK, GRP_W), lambda b, r, i: (b, r, jnp.maximum(i * sub - 1, 0), part))

    return pl.pallas_call(
        functools.partial(_attn_kernel, tq=tq),
        grid=(batch, dil, L // tq),
        in_specs=[cur(0), cur(1), cur(2), prev(1), prev(2), _const_spec(bias.shape), _const_spec(bias0.shape)],
        out_specs=[pl.BlockSpec((1, 1, tq, GRP_W), lambda b, r, i: (b, r, i, 0)),
                   pl.BlockSpec((1, 1, tq, DH), lambda b, r, i: (b, r, i, 0))],
        out_shape=[jax.ShapeDtypeStruct((batch, dil, L, GRP_W), BF16),
                   jax.ShapeDtypeStruct((batch, dil, L, DH), F32)],
        compiler_params=_cparams(("parallel", "parallel", "parallel")),
        name=f"attn_d{dil}",
    )(qkv, qkv, qkv, qkv, qkv, jnp.asarray(bias), jnp.asarray(bias0))


def _ret_kernel(q_ref, k_ref, v_ref, g_ref, gng_ref, gnb_ref, dec_ref, xi_ref, zeta_ref, gch_ref,
                o_ref, state_ref, *, chunk, nchunk):
    @pl.when(pl.program_id(2) == 0)
    def _():
        state_ref[...] = jnp.zeros_like(state_ref)

    for c in range(nchunk):
        rows = slice(c * chunk, (c + 1) * chunk)
        q = q_ref[rows, :]
        k = k_ref[rows, :]
        v = v_ref[rows, :]
        s = _nt_dot(q, k) * dec_ref[0]
        inner = jnp.dot(s.astype(BF16), v, preferred_element_type=F32)
        st = state_ref[...]
        cross = jnp.dot(q, st.astype(BF16), preferred_element_type=F32) * xi_ref[0]
        kz = (k.astype(F32) * zeta_ref[0]).astype(BF16)
        state_ref[...] = st * gch_ref[0] + _tn_dot(kz, v)
        o = inner + cross
        mu = jnp.mean(o, axis=-1, keepdims=True)
        oc = o - mu
        var = jnp.mean(oc * oc, axis=-1, keepdims=True)
        y = oc * lax.rsqrt(var + EPS) * gng_ref[...] + gnb_ref[...]
        gate = g_ref[rows, :].astype(F32)
        o_ref[rows, :] = (y * gate * (1.0 / (1.0 + jnp.exp(-gate)))).astype(BF16)


def _retention(r_mat, gn_g, gn_b, batch, seq, ts, chunk):
    T = r_mat.shape[0]
    ns = seq // ts
    H = RET_HEADS
    log_g = np.log(1.0 - 2.0 ** (-5.0 - np.arange(H, dtype=np.float64)))
    idx = np.arange(chunk, dtype=np.float64)
    diff = idx[:, None] - idx[None, :]
    scale = RET_DK ** -0.5
    dec = np.where(diff >= 0, np.exp(log_g[:, None, None] * np.maximum(diff, 0.0)), 0.0) * scale
    xi = np.exp(log_g[:, None] * (idx[None, :] + 1.0))
    zeta = np.exp(log_g[:, None] * (chunk - 1.0 - idx[None, :])) * scale
    gch = np.exp(log_g * chunk)
    dec = jnp.asarray(dec, F32)
    xi = jnp.asarray(np.broadcast_to(xi[:, :, None], (H, chunk, RET_DV)), F32)
    zeta = jnp.asarray(np.broadcast_to(zeta[:, :, None], (H, chunk, RET_DK)), F32)
    gch = jnp.asarray(np.broadcast_to(gch[:, None, None], (H, 1, RET_DV)), F32)

    return pl.pallas_call(
        functools.partial(_ret_kernel, chunk=chunk, nchunk=ts // chunk),
        grid=(batch, H, ns),
        in_specs=[
            pl.BlockSpec((ts, RET_DK), lambda b, h, i: (b * ns + i, h)),
            pl.BlockSpec((ts, RET_DK), lambda b, h, i: (b * ns + i, RET_QK_W // RET_DK + h)),
            pl.BlockSpec((ts, RET_DV), lambda b, h, i: (b * ns + i, 2 * RET_QK_W // RET_DV + h)),
            pl.BlockSpec((ts, RET_DV), lambda b, h, i: (b * ns + i, (2 * RET_QK_W + RET_V_W) // RET_DV + h)),
            pl.BlockSpec((1, RET_DV), lambda b, h, i: (0, h)),
            pl.BlockSpec((1, RET_DV), lambda b, h, i: (0, h)),
            pl.BlockSpec((1, chunk, chunk), lambda b, h, i: (h, 0, 0)),
            pl.BlockSpec((1, chunk, RET_DV), lambda b, h, i: (h, 0, 0)),
            pl.BlockSpec((1, chunk, RET_DK), lambda b, h, i: (h, 0, 0)),
            pl.BlockSpec((1, 1, RET_DV), lambda b, h, i: (h, 0, 0)),
        ],
        out_specs=pl.BlockSpec((ts, RET_DV), lambda b, h, i: (b * ns + i, h)),
        out_shape=jax.ShapeDtypeStruct((T, RET_V_W), BF16),
        scratch_shapes=[pltpu.VMEM((RET_DK, RET_DV), F32)],
        compiler_params=_cparams(("parallel", "parallel", "arbitrary")),
        name="retention",
    )(r_mat, r_mat, r_mat, r_mat, gn_g, gn_b, dec, xi, zeta, gch)


def _post_kernel(x_ref, o1_ref, o2_ref, o3_ref, l1_ref, l2_ref, l3_ref, or_ref, ga_ref, gb_ref,
                 wa_ref, wb_ref, wo_ref, g2_ref, wu_ref, wd_ref, out_ref,
                 oa_ref, so2_ref, so3_ref, sl2_ref, sl3_ref, *, tm, ff_chunk):
    for (o_ref, l_ref, so_ref, sl_ref, dil) in ((o2_ref, l2_ref, so2_ref, sl2_ref, ATT_GROUPS[1][1]),
                                                (o3_ref, l3_ref, so3_ref, sl3_ref, ATT_GROUPS[2][1])):
        n = tm // dil
        for r in range(dil):
            blk = o_ref[0, r].astype(F32)
            for h in range(HPG):
                so_ref[h, pl.ds(r, n, stride=dil), :] = blk[:, h * DH:(h + 1) * DH]
            sl_ref[pl.ds(r, n, stride=dil), :] = l_ref[0, r]

    l1, l2, l3 = l1_ref[0, 0], sl2_ref[...], sl3_ref[...]
    lm = jnp.maximum(jnp.maximum(l1, l2), l3)
    e1, e2, e3 = jnp.exp(l1 - lm), jnp.exp(l2 - lm), jnp.exp(l3 - lm)
    inv = 1.0 / (e1 + e2 + e3)
    a1, a2, a3 = e1 * inv, e2 * inv, e3 * inv
    for h in range(HPG):
        hs = slice(h * DH, (h + 1) * DH)
        c = h * (DH // HPG)
        oa = (a1[:, c:c + 1] * o1_ref[0, 0, :, hs].astype(F32)
              + a2[:, c:c + 1] * so2_ref[h]
              + a3[:, c:c + 1] * so3_ref[h])
        oa_ref[:, hs] = oa.astype(BF16)

    ya = jnp.dot(oa_ref[...], wa_ref[...], preferred_element_type=F32)
    yb = jnp.dot(or_ref[...], wb_ref[...], preferred_element_type=F32)
    sa = 1.0 / (1.0 + jnp.exp(-ga_ref[...].astype(F32)))
    sb = 1.0 / (1.0 + jnp.exp(-gb_ref[...].astype(F32)))
    y = (sa * ya + sb * yb).astype(BF16)
    x1 = x_ref[...] + jnp.dot(y, wo_ref[...], preferred_element_type=F32)
    xn2 = _rms_rows(x1, g2_ref[...]).astype(BF16)
    acc = x1
    for c in range(D_FF // ff_chunk):
        cs = slice(c * ff_chunk, (c + 1) * ff_chunk)
        hcol = jnp.maximum(jnp.dot(xn2, wu_ref[:, cs], preferred_element_type=F32), 0.0)
        acc = acc + jnp.dot((hcol * hcol).astype(BF16), wd_ref[cs, :], preferred_element_type=F32)
    out_ref[...] = acc


def _post(x2, o_att, lse_att, o_r, r_mat, wa, wb, wo, g2, wu, wd, batch, seq, tm, ff_chunk):
    T = x2.shape[0]
    nt = seq // tm
    gate_cb = (2 * RET_QK_W + 2 * RET_V_W) // D_MODEL

    def row(w, cb=0):
        return pl.BlockSpec((tm, w), lambda b, i: (b * nt + i, cb))

    def sub(w, gi):
        dil = ATT_GROUPS[gi][1]
        return pl.BlockSpec((1, dil, tm // dil, w), lambda b, i: (b, 0, i, 0))

    return pl.pallas_call(
        functools.partial(_post_kernel, tm=tm, ff_chunk=ff_chunk),
        grid=(batch, nt),
        in_specs=[row(D_MODEL), sub(GRP_W, 0), sub(GRP_W, 1), sub(GRP_W, 2), sub(DH, 0), sub(DH, 1), sub(DH, 2),
                  row(RET_V_W), row(D_MODEL, gate_cb), row(D_MODEL, gate_cb + 1),
                  _const_spec(wa.shape), _const_spec(wb.shape), _const_spec(wo.shape), _const_spec((1, D_MODEL)),
                  _const_spec(wu.shape), _const_spec(wd.shape)],
        out_specs=row(D_MODEL),
        out_shape=jax.ShapeDtypeStruct((T, D_MODEL), F32),
        scratch_shapes=[pltpu.VMEM((tm, GRP_W), BF16),
                        pltpu.VMEM((HPG, tm, DH), F32), pltpu.VMEM((HPG, tm, DH), F32),
                        pltpu.VMEM((tm, DH), F32), pltpu.VMEM((tm, DH), F32)],
        compiler_params=_cparams(("parallel", "parallel")),
        name="post",
    )(x2, *o_att, *lse_att, o_r, r_mat, r_mat, wa, wb, wo, g2, wu, wd)


def kernel(x, norm1_g, w_in, q_norm_g, k_norm_g, ret_gn_g, ret_gn_b, w_proj_a, w_proj_b, w_out, norm2_g, w_up, w_down):
    B, S, D = x.shape
    T = B * S
    depth = w_in.shape[0]
    x2 = x.reshape(T, D)
    for l in range(depth):
        w_bf = w_in[l].astype(BF16)
        g1 = norm1_g[l].reshape(1, D)
        qg = q_norm_g[l] * (DH ** -0.5)
        kg = k_norm_g[l]

        o_att, lse_att = [], []
        for gi, (_, dil) in enumerate(ATT_GROUPS):
            cols = slice(gi * GRP_W, (gi + 1) * GRP_W)
            w_g = jnp.concatenate([w_bf[:, p * ATT_W:(p + 1) * ATT_W][:, cols] for p in range(3)], axis=1)
            hs = slice(gi * HPG, (gi + 1) * HPG)
            hg = jnp.concatenate([qg[hs].reshape(1, GRP_W), kg[hs].reshape(1, GRP_W)], axis=1)
            qkv = _proj_att(x2, g1, w_g, hg, dil, B, S, tm=1024, rc=256)
            o, lse = _attention(qkv, gi, tq=512)
            o_att.append(o)
            lse_att.append(lse)

        r_mat = _proj_rest(x2, g1, w_bf[:, 3 * ATT_W:], tm=1024, tn=2048)
        o_r = _retention(r_mat, ret_gn_g[l].reshape(1, RET_V_W), ret_gn_b[l].reshape(1, RET_V_W), B, S, ts=1024, chunk=256)

        x2 = _post(x2, o_att, lse_att, o_r, r_mat,
                   w_proj_a[l].astype(BF16), w_proj_b[l].astype(BF16), w_out[l].astype(BF16),
                   norm2_g[l].reshape(1, D), w_up[l].astype(BF16), w_down[l].astype(BF16),
                   B, S, tm=512, ff_chunk=1024)
    return x2.reshape(B, S, D)
```

```python
import functools

import numpy as np
import jax
import jax.numpy as jnp
from jax import lax
from jax.experimental import pallas as pl
from jax.experimental.pallas import tpu as pltpu

F32 = jnp.float32
BF16 = jnp.bfloat16

D_MODEL = 1024
ATT_GROUPS = ((128, 1), (512, 4), (2048, 16))
HPG = 4
ATT_HEADS = 12
DH = 128
LANE = 128
NSLAB = D_MODEL // LANE
ATT_BLOCK = 128
ATT_W = ATT_HEADS * DH
GRP_W = HPG * DH
RET_HEADS = 4
RET_DK = 256
RET_DV = 512
RET_QK_W = 1024
RET_V_W = 2048
D_FF = 4096
EPS = 1e-6
NEG = -1e30

VMEM_LIMIT = 56 * 1024 * 1024


def _cparams(sem):
    return pltpu.CompilerParams(dimension_semantics=sem, vmem_limit_bytes=VMEM_LIMIT)


def _const_spec(shape):
    nd = len(shape)
    return pl.BlockSpec(shape, lambda *_: (0,) * nd, pipeline_mode=pl.Buffered(1))


def _nt_dot(a, b):
    return lax.dot_general(a, b, (((1,), (1,)), ((), ())), preferred_element_type=F32)


def _tn_dot(a, b):
    return lax.dot_general(a, b, (((0,), (0,)), ((), ())), preferred_element_type=F32)


def _rms_rows(x, g):
    return x * lax.rsqrt(jnp.mean(x * x, axis=-1, keepdims=True) + EPS) * g


def _proj_att_kernel(*refs, dil, tm, rc):
    x_refs = refs[:NSLAB]
    g1_ref, w_ref, hg_ref, o_ref = refs[NSLAB:NSLAB + 4]
    n = tm // dil
    if dil == 16:
        xa_ref, xg_ref = refs[NSLAB + 4:]
        n4 = tm // 4
        for c in range(NSLAB):
            for lo in range(4):
                xa_ref[c, lo * n4:(lo + 1) * n4, :] = x_refs[c][pl.ds(lo, n4, stride=4), :]
        for c in range(NSLAB):
            for r in range(dil):
                xg_ref[c, r * n:(r + 1) * n, :] = xa_ref[c, pl.ds((r % 4) * n4 + r // 4, n, stride=4), :]

    def load_rows(c, r, l0, ln):
        if dil == 1:
            return x_refs[c][l0:l0 + ln, :]
        if dil == 16:
            return xg_ref[c, r * n + l0:r * n + l0 + ln, :]
        return x_refs[c][pl.ds(r + l0 * dil, ln, stride=dil), :]

    for kc in range(tm // rc):
        segs, p = [], kc * rc
        while p < (kc + 1) * rc:
            r, l0 = divmod(p, n)
            ln = min(n - l0, (kc + 1) * rc - p)
            segs.append((r, l0, ln))
            p += ln
        pieces = []
        for (r, l0, ln) in segs:
            xs = [load_rows(c, r, l0, ln) for c in range(NSLAB)]
            ss = xs[0] * xs[0]
            for c in range(1, NSLAB):
                ss = ss + xs[c] * xs[c]
            scale = lax.rsqrt(jnp.sum(ss, axis=-1, keepdims=True) * (1.0 / D_MODEL) + EPS)
            pieces.append(jnp.concatenate(
                [(xs[c] * scale * g1_ref[:, c * LANE:(c + 1) * LANE]).astype(BF16) for c in range(NSLAB)], axis=1))
        xn = pieces[0] if len(pieces) == 1 else jnp.concatenate(pieces, axis=0)
        acc = jnp.dot(xn, w_ref[...], preferred_element_type=F32)
        outs = []
        for h in range(2 * HPG):
            hs = slice(h * DH, (h + 1) * DH)
            t = acc[:, hs]
            rr = lax.rsqrt(jnp.sum(t * t, axis=-1, keepdims=True) * (1.0 / DH) + EPS)
            outs.append((t * rr * hg_ref[:, hs]).astype(BF16))
        outs.append(acc[:, 2 * GRP_W:].astype(BF16))
        res = jnp.concatenate(outs, axis=1)
        off = 0
        for (r, l0, ln) in segs:
            o_ref[0, r, l0:l0 + ln, :] = res[off:off + ln]
            off += ln


def _proj_att(x2, g1, w_g, hg, dil, batch, seq, tm, rc):
    nt = seq // tm
    slab = [pl.BlockSpec((tm, LANE), functools.partial(lambda b, i, c: (b * nt + i, c), c=c)) for c in range(NSLAB)]
    return pl.pallas_call(
        functools.partial(_proj_att_kernel, dil=dil, tm=tm, rc=rc),
        grid=(batch, nt),
        in_specs=slab + [_const_spec((1, D_MODEL)), _const_spec(w_g.shape), _const_spec(hg.shape)],
        out_specs=pl.BlockSpec((1, dil, tm // dil, 3 * GRP_W), lambda b, i: (b, 0, i, 0)),
        out_shape=jax.ShapeDtypeStruct((batch, dil, seq // dil, 3 * GRP_W), BF16),
        scratch_shapes=[pltpu.VMEM((NSLAB, tm, LANE), F32)] * 2 if dil == 16 else [],
        compiler_params=_cparams(("parallel", "parallel")),
        name=f"proj_att_d{dil}",
    )(*([x2] * NSLAB), g1, w_g, hg)


def _sigmoid(t):
    return 1.0 / (1.0 + jnp.exp(-t))


def _proj_rest_kernel(x_ref, g1_ref, w_ref, o_ref, *, tm, rc, tn):
    silu_lo = 2 * RET_QK_W + RET_V_W
    sig_lo = silu_lo + RET_V_W
    for kc in range(tm // rc):
        rows = slice(kc * rc, (kc + 1) * rc)
        xn = _rms_rows(x_ref[rows, :], g1_ref[...]).astype(BF16)
        for ct in range(o_ref.shape[1] // tn):
            cs = slice(ct * tn, (ct + 1) * tn)
            acc = jnp.dot(xn, w_ref[:, cs], preferred_element_type=F32)
            if ct * tn >= sig_lo:
                acc = _sigmoid(acc)
            elif ct * tn >= silu_lo:
                acc = acc * _sigmoid(acc)
            o_ref[rows, cs] = acc.astype(BF16)


def _proj_rest(x2, g1, w_rest, tm, rc, tn):
    T = x2.shape[0]
    n = w_rest.shape[1]
    return pl.pallas_call(
        functools.partial(_proj_rest_kernel, tm=tm, rc=rc, tn=tn),
        grid=(T // tm,),
        in_specs=[
            pl.BlockSpec((tm, D_MODEL), lambda i: (i, 0)),
            _const_spec((1, D_MODEL)),
            _const_spec(w_rest.shape),
        ],
        out_specs=pl.BlockSpec((tm, n), lambda i: (i, 0)),
        out_shape=jax.ShapeDtypeStruct((T, n), BF16),
        compiler_params=_cparams(("parallel",)),
        name="proj_rest",
    )(x2, g1, w_rest)


def _attn_kernel(q_ref, kc_ref, vc_ref, kp_ref, vp_ref, bias_ref, bias0_ref, o_ref, lse_ref, *, tq):
    first = pl.program_id(2) == 0
    lane = lax.broadcasted_iota(jnp.int32, (ATT_BLOCK, DH), 1)
    for jb in range(tq // ATT_BLOCK):
        rows = slice(jb * ATT_BLOCK, (jb + 1) * ATT_BLOCK)
        lse_blk = None
        for h in range(HPG):
            hs = slice(h * DH, (h + 1) * DH)
            q = q_ref[0, 0, rows, hs]
            if jb == 0:
                k = jnp.concatenate([kp_ref[0, 0, :, hs], kc_ref[0, 0, 0:ATT_BLOCK, hs]], axis=0)
                v = jnp.concatenate([vp_ref[0, 0, :, hs], vc_ref[0, 0, 0:ATT_BLOCK, hs]], axis=0)
                b = jnp.where(first, bias0_ref[h], bias_ref[h])
            else:
                kr = slice((jb - 1) * ATT_BLOCK, (jb + 1) * ATT_BLOCK)
                k = kc_ref[0, 0, kr, hs]
                v = vc_ref[0, 0, kr, hs]
                b = bias_ref[h]
            s = _nt_dot(q, k) + b
            m = jnp.max(s, axis=-1, keepdims=True)
            p = jnp.exp(s - m)
            den = jnp.sum(p, axis=-1, keepdims=True)
            o = jnp.dot(p.astype(BF16), v, preferred_element_type=F32) * (1.0 / den)
            o_ref[0, 0, rows, hs] = o.astype(BF16)
            lse = jnp.broadcast_to(m + jnp.log(den), (ATT_BLOCK, DH))
            lse_blk = lse if h == 0 else jnp.where(lane >= h * (DH // HPG), lse, lse_blk)
        lse_ref[0, 0, rows, :] = lse_blk


def _attention(qkv, gi, tq):
    window, dil = ATT_GROUPS[gi]
    batch, _, L, _ = qkv.shape
    sub = tq // ATT_BLOCK

    slopes = 2.0 ** (-8.0 * np.arange(1, ATT_HEADS + 1, dtype=np.float32) / ATT_HEADS)
    slopes = slopes[gi * HPG:(gi + 1) * HPG]
    qi = np.arange(ATT_BLOCK)[:, None]
    kj = np.arange(2 * ATT_BLOCK)[None, :]
    dist = ATT_BLOCK + qi - kj
    valid = (dist >= 0) & (dist <= window // dil)
    bias = np.where(valid[None], -slopes[:, None, None] * (dist * dil).astype(np.float32)[None], NEG).astype(np.float32)
    bias0 = np.where((kj >= ATT_BLOCK)[None], bias, NEG).astype(np.float32)

    def cur(part):
        return pl.BlockSpec((1, 1, tq, GRP_W), lambda b, r, i: (b, r, i, part))

    def prev(part):
        return pl.BlockSpec((1, 1, ATT_BLOCK, GRP_W), lambda b, r, i: (b, r, jnp.maximum(i * sub - 1, 0), part))

    return pl.pallas_call(
        functools.partial(_attn_kernel, tq=tq),
        grid=(batch, dil, L // tq),
        in_specs=[cur(0), cur(1), cur(2), prev(1), prev(2), _const_spec(bias.shape), _const_spec(bias0.shape)],
        out_specs=[pl.BlockSpec((1, 1, tq, GRP_W), lambda b, r, i: (b, r, i, 0)),
                   pl.BlockSpec((1, 1, tq, DH), lambda b, r, i: (b, r, i, 0))],
        out_shape=[jax.ShapeDtypeStruct((batch, dil, L, GRP_W), BF16),
                   jax.ShapeDtypeStruct((batch, dil, L, DH), F32)],
        compiler_params=_cparams(("parallel", "parallel", "parallel")),
        name=f"attn_d{dil}",
    )(qkv, qkv, qkv, qkv, qkv, jnp.asarray(bias), jnp.asarray(bias0))


def _ret_kernel(q_ref, k_ref, v_ref, g_ref, gng_ref, gnb_ref, dec_ref, xi_ref, zeta_ref, gch_ref,
                o_ref, state_ref, *, chunk, nchunk):
    @pl.when(pl.program_id(2) == 0)
    def _():
        state_ref[...] = jnp.zeros_like(state_ref)

    for c in range(nchunk):
        rows = slice(c * chunk, (c + 1) * chunk)
        q = q_ref[rows, :]
        k = k_ref[rows, :]
        v = v_ref[rows, :]
        s = _nt_dot(q, k) * dec_ref[0]
        inner = jnp.dot(s.astype(BF16), v, preferred_element_type=F32)
        st = state_ref[...]
        cross = jnp.dot(q, st.astype(BF16), preferred_element_type=F32) * xi_ref[0]
        kz = (k.astype(F32) * zeta_ref[0]).astype(BF16)
        state_ref[...] = st * gch_ref[0] + _tn_dot(kz, v)
        o = inner + cross
        mu = jnp.mean(o, axis=-1, keepdims=True)
        oc = o - mu
        var = jnp.mean(oc * oc, axis=-1, keepdims=True)
        y = oc * lax.rsqrt(var + EPS) * gng_ref[...] + gnb_ref[...]
        o_ref[rows, :] = (y * g_ref[rows, :].astype(F32)).astype(BF16)


def _retention(r_mat, gn_g, gn_b, batch, seq, ts, chunk):
    T = r_mat.shape[0]
    ns = seq // ts
    H = RET_HEADS
    log_g = np.log(1.0 - 2.0 ** (-5.0 - np.arange(H, dtype=np.float64)))
    idx = np.arange(chunk, dtype=np.float64)
    diff = idx[:, None] - idx[None, :]
    scale = RET_DK ** -0.5
    dec = np.where(diff >= 0, np.exp(log_g[:, None, None] * np.maximum(diff, 0.0)), 0.0) * scale
    xi = np.exp(log_g[:, None] * (idx[None, :] + 1.0))
    zeta = np.exp(log_g[:, None] * (chunk - 1.0 - idx[None, :])) * scale
    gch = np.exp(log_g * chunk)
    dec = jnp.asarray(dec, F32)
    xi = jnp.asarray(np.broadcast_to(xi[:, :, None], (H, chunk, RET_DV)), F32)
    zeta = jnp.asarray(np.broadcast_to(zeta[:, :, None], (H, chunk, RET_DK)), F32)
    gch = jnp.asarray(np.broadcast_to(gch[:, None, None], (H, 1, RET_DV)), F32)

    return pl.pallas_call(
        functools.partial(_ret_kernel, chunk=chunk, nchunk=ts // chunk),
        grid=(batch, H, ns),
        in_specs=[
            pl.BlockSpec((ts, RET_DK), lambda b, h, i: (b * ns + i, h)),
            pl.BlockSpec((ts, RET_DK), lambda b, h, i: (b * ns + i, RET_QK_W // RET_DK + h)),
            pl.BlockSpec((ts, RET_DV), lambda b, h, i: (b * ns + i, 2 * RET_QK_W // RET_DV + h)),
            pl.BlockSpec((ts, RET_DV), lambda b, h, i: (b * ns + i, (2 * RET_QK_W + RET_V_W) // RET_DV + h)),
            pl.BlockSpec((1, RET_DV), lambda b, h, i: (0, h)),
            pl.BlockSpec((1, RET_DV), lambda b, h, i: (0, h)),
            pl.BlockSpec((1, chunk, chunk), lambda b, h, i: (h, 0, 0)),
            pl.BlockSpec((1, chunk, RET_DV), lambda b, h, i: (h, 0, 0)),
            pl.BlockSpec((1, chunk, RET_DK), lambda b, h, i: (h, 0, 0)),
            pl.BlockSpec((1, 1, RET_DV), lambda b, h, i: (h, 0, 0)),
        ],
        out_specs=pl.BlockSpec((ts, RET_DV), lambda b, h, i: (b * ns + i, h)),
        out_shape=jax.ShapeDtypeStruct((T, RET_V_W), BF16),
        scratch_shapes=[pltpu.VMEM((RET_DK, RET_DV), F32)],
        compiler_params=_cparams(("parallel", "parallel", "arbitrary")),
        name="retention",
    )(r_mat, r_mat, r_mat, r_mat, gn_g, gn_b, dec, xi, zeta, gch)


def _post_kernel(x_ref, o1_ref, o2_ref, o3_ref, l1_ref, l2_ref, l3_ref, or_ref, ga_ref, gb_ref,
                 wa_ref, wb_ref, wo_ref, g2_ref, wu_ref, wd_ref, out_ref,
                 oa_ref, so2_ref, so3_ref, sl2_ref, sl3_ref, *, tm, rc, ff_chunk):
    for kc in range(tm // rc):
        rows = slice(kc * rc, (kc + 1) * rc)
        for (o_ref, l_ref, so_ref, sl_ref, dil) in ((o2_ref, l2_ref, so2_ref, sl2_ref, ATT_GROUPS[1][1]),
                                                    (o3_ref, l3_ref, so3_ref, sl3_ref, ATT_GROUPS[2][1])):
            n = rc // dil
            src = slice(kc * n, (kc + 1) * n)
            for r in range(dil):
                blk = o_ref[0, r, src, :].astype(F32)
                for h in range(HPG):
                    so_ref[h, pl.ds(kc * rc + r, n, stride=dil), :] = blk[:, h * DH:(h + 1) * DH]
                sl_ref[pl.ds(kc * rc + r, n, stride=dil), :] = l_ref[0, r, src, :]

        l1, l2, l3 = l1_ref[0, 0, rows, :], sl2_ref[rows, :], sl3_ref[rows, :]
        lm = jnp.maximum(jnp.maximum(l1, l2), l3)
        e1, e2, e3 = jnp.exp(l1 - lm), jnp.exp(l2 - lm), jnp.exp(l3 - lm)
        inv = 1.0 / (e1 + e2 + e3)
        a1, a2, a3 = e1 * inv, e2 * inv, e3 * inv
        for h in range(HPG):
            hs = slice(h * DH, (h + 1) * DH)
            c = h * (DH // HPG)
            oa = (a1[:, c:c + 1] * o1_ref[0, 0, rows, hs].astype(F32)
                  + a2[:, c:c + 1] * so2_ref[h, rows, :]
                  + a3[:, c:c + 1] * so3_ref[h, rows, :])
            oa_ref[rows, hs] = oa.astype(BF16)

        ya = jnp.dot(oa_ref[rows, :], wa_ref[...], preferred_element_type=F32)
        yb = jnp.dot(or_ref[rows, :], wb_ref[...], preferred_element_type=F32)
        y = (ga_ref[rows, :].astype(F32) * ya + gb_ref[rows, :].astype(F32) * yb).astype(BF16)
        x1 = x_ref[rows, :] + jnp.dot(y, wo_ref[...], preferred_element_type=F32)
        xn2 = _rms_rows(x1, g2_ref[...]).astype(BF16)
        acc = x1
        for c in range(D_FF // ff_chunk):
            cs = slice(c * ff_chunk, (c + 1) * ff_chunk)
            hcol = jnp.maximum(jnp.dot(xn2, wu_ref[:, cs], preferred_element_type=F32), 0.0)
            acc = acc + jnp.dot((hcol * hcol).astype(BF16), wd_ref[cs, :], preferred_element_type=F32)
        out_ref[rows, :] = acc


def _post(x2, o_att, lse_att, o_r, r_mat, wa, wb, wo, g2, wu, wd, batch, seq, tm, rc, ff_chunk):
    T = x2.shape[0]
    nt = seq // tm
    gate_cb = (2 * RET_QK_W + 2 * RET_V_W) // D_MODEL

    def row(w, cb=0):
        return pl.BlockSpec((tm, w), lambda b, i: (b * nt + i, cb))

    def sub(w, gi):
        dil = ATT_GROUPS[gi][1]
        return pl.BlockSpec((1, dil, tm // dil, w), lambda b, i: (b, 0, i, 0))

    return pl.pallas_call(
        functools.partial(_post_kernel, tm=tm, rc=rc, ff_chunk=ff_chunk),
        grid=(batch, nt),
        in_specs=[row(D_MODEL), sub(GRP_W, 0), sub(GRP_W, 1), sub(GRP_W, 2), sub(DH, 0), sub(DH, 1), sub(DH, 2),
                  row(RET_V_W), row(D_MODEL, gate_cb), row(D_MODEL, gate_cb + 1),
                  _const_spec(wa.shape), _const_spec(wb.shape), _const_spec(wo.shape), _const_spec((1, D_MODEL)),
                  _const_spec(wu.shape), _const_spec(wd.shape)],
        out_specs=row(D_MODEL),
        out_shape=jax.ShapeDtypeStruct((T, D_MODEL), F32),
        scratch_shapes=[pltpu.VMEM((tm, GRP_W), BF16),
                        pltpu.VMEM((HPG, tm, DH), F32), pltpu.VMEM((HPG, tm, DH), F32),
                        pltpu.VMEM((tm, DH), F32), pltpu.VMEM((tm, DH), F32)],
        compiler_params=_cparams(("parallel", "parallel")),
        name="post",
    )(x2, *o_att, *lse_att, o_r, r_mat, r_mat, wa, wb, wo, g2, wu, wd)


def kernel(x, norm1_g, w_in, q_norm_g, k_norm_g, ret_gn_g, ret_gn_b, w_proj_a, w_proj_b, w_out, norm2_g, w_up, w_down):
    B, S, D = x.shape
    T = B * S
    depth = w_in.shape[0]
    x2 = x.reshape(T, D)
    for l in range(depth):
        w_bf = w_in[l].astype(BF16)
        g1 = norm1_g[l].reshape(1, D)
        qg = q_norm_g[l] * (DH ** -0.5)
        kg = k_norm_g[l]

        o_att, lse_att = [], []
        for gi, (_, dil) in enumerate(ATT_GROUPS):
            cols = slice(gi * GRP_W, (gi + 1) * GRP_W)
            w_g = jnp.concatenate([w_bf[:, p * ATT_W:(p + 1) * ATT_W][:, cols] for p in range(3)], axis=1)
            hs = slice(gi * HPG, (gi + 1) * HPG)
            hg = jnp.concatenate([qg[hs].reshape(1, GRP_W), kg[hs].reshape(1, GRP_W)], axis=1)
            qkv = _proj_att(x2, g1, w_g, hg, dil, B, S, tm=1024, rc=256)
            o, lse = _attention(qkv, gi, tq=512)
            o_att.append(o)
            lse_att.append(lse)

        r_mat = _proj_rest(x2, g1, w_bf[:, 3 * ATT_W:], tm=512, rc=256, tn=1024)
        o_r = _retention(r_mat, ret_gn_g[l].reshape(1, RET_V_W), ret_gn_b[l].reshape(1, RET_V_W), B, S, ts=1024, chunk=256)

        x2 = _post(x2, o_att, lse_att, o_r, r_mat,
                   w_proj_a[l].astype(BF16), w_proj_b[l].astype(BF16), w_out[l].astype(BF16),
                   norm2_g[l].reshape(1, D), w_up[l].astype(BF16), w_down[l].astype(BF16),
                   B, S, tm=512, rc=256, ff_chunk=1024)
    return x2.reshape(B, S, D)
```

```python
import functools

import numpy as np
import jax
import jax.numpy as jnp
from jax import lax
from jax.experimental import pallas as pl
from jax.experimental.pallas import tpu as pltpu

F32 = jnp.float32
BF16 = jnp.bfloat16

D_MODEL = 1024
ATT_GROUPS = ((128, 1), (512, 4), (2048, 16))
HPG = 4
ATT_HEADS = 12
DH = 128
LANE = 128
NSLAB = D_MODEL // LANE
ATT_BLOCK = 128
ATT_W = ATT_HEADS * DH
GRP_W = HPG * DH
STAT_W = LANE // HPG
RET_HEADS = 4
RET_DK = 256
RET_DV = 512
RET_QK_W = 1024
RET_V_W = 2048
D_FF = 4096
EPS = 1e-6
NEG = -1e30
LOG2E = float(np.log2(np.e))

VMEM_LIMIT = 56 * 1024 * 1024


def _cparams(sem):
    return pltpu.CompilerParams(dimension_semantics=sem, vmem_limit_bytes=VMEM_LIMIT)


def _const_spec(shape):
    nd = len(shape)
    return pl.BlockSpec(shape, lambda *_: (0,) * nd, pipeline_mode=pl.Buffered(1))


def _nt_dot(a, b):
    return lax.dot_general(a, b, (((1,), (1,)), ((), ())), preferred_element_type=F32)


def _tn_dot(a, b):
    return lax.dot_general(a, b, (((0,), (0,)), ((), ())), preferred_element_type=F32)


def _rms_rows(x, g):
    return x * lax.rsqrt(jnp.mean(x * x, axis=-1, keepdims=True) + EPS) * g


def _proj_att_kernel(*refs, dil, tm, rc):
    x_refs = refs[:NSLAB]
    g1_ref, w_ref, hg_ref, o_ref = refs[NSLAB:NSLAB + 4]
    n = tm // dil
    if dil == 16:
        xa_ref, xg_ref = refs[NSLAB + 4:]
        n4 = tm // 4
        for c in range(NSLAB):
            for lo in range(4):
                xa_ref[c, lo * n4:(lo + 1) * n4, :] = x_refs[c][pl.ds(lo, n4, stride=4), :]
        for c in range(NSLAB):
            for r in range(dil):
                xg_ref[c, r * n:(r + 1) * n, :] = xa_ref[c, pl.ds((r % 4) * n4 + r // 4, n, stride=4), :]

    def load_rows(c, r, l0, ln):
        if dil == 1:
            return x_refs[c][l0:l0 + ln, :]
        if dil == 16:
            return xg_ref[c, r * n + l0:r * n + l0 + ln, :]
        return x_refs[c][pl.ds(r + l0 * dil, ln, stride=dil), :]

    for kc in range(tm // rc):
        segs, p = [], kc * rc
        while p < (kc + 1) * rc:
            r, l0 = divmod(p, n)
            ln = min(n - l0, (kc + 1) * rc - p)
            segs.append((r, l0, ln))
            p += ln
        pieces = []
        for (r, l0, ln) in segs:
            xs = [load_rows(c, r, l0, ln) for c in range(NSLAB)]
            ss = xs[0] * xs[0]
            for c in range(1, NSLAB):
                ss = ss + xs[c] * xs[c]
            scale = lax.rsqrt(jnp.sum(ss, axis=-1, keepdims=True) * (1.0 / D_MODEL) + EPS)
            pieces.append(jnp.concatenate(
                [(xs[c] * scale * g1_ref[:, c * LANE:(c + 1) * LANE]).astype(BF16) for c in range(NSLAB)], axis=1))
        xn = pieces[0] if len(pieces) == 1 else jnp.concatenate(pieces, axis=0)
        acc = jnp.dot(xn, w_ref[...], preferred_element_type=F32)
        outs = []
        for h in range(2 * HPG):
            hs = slice(h * DH, (h + 1) * DH)
            t = acc[:, hs]
            rr = lax.rsqrt(jnp.sum(t * t, axis=-1, keepdims=True) * (1.0 / DH) + EPS)
            outs.append((t * rr * hg_ref[:, hs]).astype(BF16))
        outs.append(acc[:, 2 * GRP_W:].astype(BF16))
        res = jnp.concatenate(outs, axis=1)
        off = 0
        for (r, l0, ln) in segs:
            o_ref[0, r, l0:l0 + ln, :] = res[off:off + ln]
            off += ln


def _proj_att(x2, g1, w_g, hg, dil, batch, seq, tm, rc):
    nt = seq // tm
    slab = [pl.BlockSpec((tm, LANE), functools.partial(lambda b, i, c: (b * nt + i, c), c=c)) for c in range(NSLAB)]
    return pl.pallas_call(
        functools.partial(_proj_att_kernel, dil=dil, tm=tm, rc=rc),
        grid=(batch, nt),
        in_specs=slab + [_const_spec((1, D_MODEL)), _const_spec(w_g.shape), _const_spec(hg.shape)],
        out_specs=pl.BlockSpec((1, dil, tm // dil, 3 * GRP_W), lambda b, i: (b, 0, i, 0)),
        out_shape=jax.ShapeDtypeStruct((batch, dil, seq // dil, 3 * GRP_W), BF16),
        scratch_shapes=[pltpu.VMEM((NSLAB, tm, LANE), F32)] * 2 if dil == 16 else [],
        compiler_params=_cparams(("parallel", "parallel")),
        name=f"proj_att_d{dil}",
    )(*([x2] * NSLAB), g1, w_g, hg)


def _sigmoid(t):
    return 1.0 / (1.0 + jnp.exp(-t))


def _proj_rest_kernel(x_ref, g1_ref, w_ref, o_ref, *, tm, rc, tn):
    silu_lo = 2 * RET_QK_W + RET_V_W
    sig_lo = silu_lo + RET_V_W
    for kc in range(tm // rc):
        rows = slice(kc * rc, (kc + 1) * rc)
        xn = _rms_rows(x_ref[rows, :], g1_ref[...]).astype(BF16)
        for ct in range(o_ref.shape[1] // tn):
            cs = slice(ct * tn, (ct + 1) * tn)
            acc = jnp.dot(xn, w_ref[:, cs], preferred_element_type=F32)
            if ct * tn >= sig_lo:
                acc = _sigmoid(acc)
            elif ct * tn >= silu_lo:
                acc = acc * _sigmoid(acc)
            o_ref[rows, cs] = acc.astype(BF16)


def _proj_rest(x2, g1, w_rest, tm, rc, tn):
    T = x2.shape[0]
    n = w_rest.shape[1]
    return pl.pallas_call(
        functools.partial(_proj_rest_kernel, tm=tm, rc=rc, tn=tn),
        grid=(T // tm,),
        in_specs=[
            pl.BlockSpec((tm, D_MODEL), lambda i: (i, 0)),
            _const_spec((1, D_MODEL)),
            _const_spec(w_rest.shape),
        ],
        out_specs=pl.BlockSpec((tm, n), lambda i: (i, 0)),
        out_shape=jax.ShapeDtypeStruct((T, n), BF16),
        compiler_params=_cparams(("parallel",)),
        name="proj_rest",
    )(x2, g1, w_rest)


def _attn_kernel(q_ref, kc_ref, vc_ref, kp_ref, vp_ref, bias_ref, bias0_ref, o_ref, st_ref, *, tq, nr):
    first = pl.program_id(2) == 0
    lane = lax.broadcasted_iota(jnp.int32, (ATT_BLOCK, DH), 1)
    ones = jnp.ones((2 * ATT_BLOCK, DH), BF16)
    for rr in range(nr):
        for jb in range(tq // ATT_BLOCK):
            rows = slice(jb * ATT_BLOCK, (jb + 1) * ATT_BLOCK)
            st_blk = None
            for h in range(HPG):
                hs = slice(h * DH, (h + 1) * DH)
                q = q_ref[0, rr, rows, hs]
                if jb == 0:
                    k = jnp.concatenate([kp_ref[0, rr, :, hs], kc_ref[0, rr, 0:ATT_BLOCK, hs]], axis=0)
                    v = jnp.concatenate([vp_ref[0, rr, :, hs], vc_ref[0, rr, 0:ATT_BLOCK, hs]], axis=0)
                    b = jnp.where(first, bias0_ref[h], bias_ref[h])
                else:
                    kr = slice((jb - 1) * ATT_BLOCK, (jb + 1) * ATT_BLOCK)
                    k = kc_ref[0, rr, kr, hs]
                    v = vc_ref[0, rr, kr, hs]
                    b = bias_ref[h]
                s = _nt_dot(q, k) + b
                m = jnp.max(s, axis=-1, keepdims=True)
                p = jnp.exp2(s - m).astype(BF16)
                oe = jnp.dot(p, jnp.concatenate([v, ones], axis=1), preferred_element_type=F32)
                o_ref[0, rr, rows, hs] = oe[:, :DH].astype(BF16)
                lo = h * STAT_W
                mb = jnp.broadcast_to(m, (ATT_BLOCK, DH))
                st_blk = mb if h == 0 else jnp.where(lane >= lo, mb, st_blk)
                st_blk = jnp.where(lane >= lo + STAT_W // 2, oe[:, DH:], st_blk)
            st_ref[0, rr, rows, :] = st_blk


def _attention(qkv, gi, tq, nr):
    window, dil = ATT_GROUPS[gi]
    batch, _, L, _ = qkv.shape
    sub = tq // ATT_BLOCK

    slopes = 2.0 ** (-8.0 * np.arange(1, ATT_HEADS + 1, dtype=np.float32) / ATT_HEADS)
    slopes = slopes[gi * HPG:(gi + 1) * HPG]
    qi = np.arange(ATT_BLOCK)[:, None]
    kj = np.arange(2 * ATT_BLOCK)[None, :]
    dist = ATT_BLOCK + qi - kj
    valid = (dist >= 0) & (dist <= window // dil)
    alibi = -slopes[:, None, None] * (dist * dil).astype(np.float32)[None] * np.float32(LOG2E)
    bias = np.where(valid[None], alibi, NEG).astype(np.float32)
    bias0 = np.where((kj >= ATT_BLOCK)[None], bias, NEG).astype(np.float32)

    def cur(part):
        return pl.BlockSpec((1, nr, tq, GRP_W), lambda b, r, i: (b, r, i, part))

    def prev(part):
        return pl.BlockSpec((1, nr, ATT_BLOCK, GRP_W), lambda b, r, i: (b, r, jnp.maximum(i * sub - 1, 0), part))

    return pl.pallas_call(
        functools.partial(_attn_kernel, tq=tq, nr=nr),
        grid=(batch, dil // nr, L // tq),
        in_specs=[cur(0), cur(1), cur(2), prev(1), prev(2), _const_spec(bias.shape), _const_spec(bias0.shape)],
        out_specs=[pl.BlockSpec((1, nr, tq, GRP_W), lambda b, r, i: (b, r, i, 0)),
                   pl.BlockSpec((1, nr, tq, DH), lambda b, r, i: (b, r, i, 0))],
        out_shape=[jax.ShapeDtypeStruct((batch, dil, L, GRP_W), BF16),
                   jax.ShapeDtypeStruct((batch, dil, L, DH), F32)],
        compiler_params=_cparams(("parallel", "parallel", "parallel")),
        name=f"attn_d{dil}",
    )(qkv, qkv, qkv, qkv, qkv, jnp.asarray(bias), jnp.asarray(bias0))


def _ret_kernel(q_ref, k_ref, v_ref, g_ref, gng_ref, gnb_ref, dec_ref, xi_ref, zeta_ref, gch_ref,
                o_ref, state_ref, *, chunk, nchunk):
    @pl.when(pl.program_id(2) == 0)
    def _():
        state_ref[...] = jnp.zeros_like(state_ref)

    for c in range(nchunk):
        rows = slice(c * chunk, (c + 1) * chunk)
        q = q_ref[rows, :]
        k = k_ref[rows, :]
        v = v_ref[rows, :]
        s = _nt_dot(q, k) * dec_ref[0]
        inner = jnp.dot(s.astype(BF16), v, preferred_element_type=F32)
        st = state_ref[...]
        cross = jnp.dot(q, st.astype(BF16), preferred_element_type=F32) * xi_ref[0]
        kz = (k.astype(F32) * zeta_ref[0]).astype(BF16)
        state_ref[...] = st * gch_ref[0] + _tn_dot(kz, v)
        o = inner + cross
        mu = jnp.mean(o, axis=-1, keepdims=True)
        oc = o - mu
        var = jnp.mean(oc * oc, axis=-1, keepdims=True)
        y = oc * lax.rsqrt(var + EPS) * gng_ref[...] + gnb_ref[...]
        o_ref[rows, :] = (y * g_ref[rows, :].astype(F32)).astype(BF16)


def _retention(r_mat, gn_g, gn_b, batch, seq, ts, chunk):
    T = r_mat.shape[0]
    ns = seq // ts
    H = RET_HEADS
    log_g = np.log(1.0 - 2.0 ** (-5.0 - np.arange(H, dtype=np.float64)))
    idx = np.arange(chunk, dtype=np.float64)
    diff = idx[:, None] - idx[None, :]
    scale = RET_DK ** -0.5
    dec = np.where(diff >= 0, np.exp(log_g[:, None, None] * np.maximum(diff, 0.0)), 0.0) * scale
    xi = np.exp(log_g[:, None] * (idx[None, :] + 1.0))
    zeta = np.exp(log_g[:, None] * (chunk - 1.0 - idx[None, :])) * scale
    gch = np.exp(log_g * chunk)
    dec = jnp.asarray(dec, F32)
    xi = jnp.asarray(np.broadcast_to(xi[:, :, None], (H, chunk, RET_DV)), F32)
    zeta = jnp.asarray(np.broadcast_to(zeta[:, :, None], (H, chunk, RET_DK)), F32)
    gch = jnp.asarray(np.broadcast_to(gch[:, None, None], (H, 1, RET_DV)), F32)

    return pl.pallas_call(
        functools.partial(_ret_kernel, chunk=chunk, nchunk=ts // chunk),
        grid=(batch, H, ns),
        in_specs=[
            pl.BlockSpec((ts, RET_DK), lambda b, h, i: (b * ns + i, h)),
            pl.BlockSpec((ts, RET_DK), lambda b, h, i: (b * ns + i, RET_QK_W // RET_DK + h)),
            pl.BlockSpec((ts, RET_DV), lambda b, h, i: (b * ns + i, 2 * RET_QK_W // RET_DV + h)),
            pl.BlockSpec((ts, RET_DV), lambda b, h, i: (b * ns + i, (2 * RET_QK_W + RET_V_W) // RET_DV + h)),
            pl.BlockSpec((1, RET_DV), lambda b, h, i: (0, h)),
            pl.BlockSpec((1, RET_DV), lambda b, h, i: (0, h)),
            pl.BlockSpec((1, chunk, chunk), lambda b, h, i: (h, 0, 0)),
            pl.BlockSpec((1, chunk, RET_DV), lambda b, h, i: (h, 0, 0)),
            pl.BlockSpec((1, chunk, RET_DK), lambda b, h, i: (h, 0, 0)),
            pl.BlockSpec((1, 1, RET_DV), lambda b, h, i: (h, 0, 0)),
        ],
        out_specs=pl.BlockSpec((ts, RET_DV), lambda b, h, i: (b * ns + i, h)),
        out_shape=jax.ShapeDtypeStruct((T, RET_V_W), BF16),
        scratch_shapes=[pltpu.VMEM((RET_DK, RET_DV), F32)],
        compiler_params=_cparams(("parallel", "parallel", "arbitrary")),
        name="retention",
    )(r_mat, r_mat, r_mat, r_mat, gn_g, gn_b, dec, xi, zeta, gch)


def _post_kernel(x_ref, o1_ref, o2_ref, o3_ref, l1_ref, l2_ref, l3_ref, or_ref, ga_ref, gb_ref,
                 wa_ref, wb_ref, wo_ref, g2_ref, wu_ref, wd_ref, out_ref,
                 oa_ref, so2_ref, so3_ref, sl2_ref, sl3_ref, *, tm, rc, ff_chunk):
    for kc in range(tm // rc):
        rows = slice(kc * rc, (kc + 1) * rc)
        for (o_ref, l_ref, so_ref, sl_ref, dil) in ((o2_ref, l2_ref, so2_ref, sl2_ref, ATT_GROUPS[1][1]),
                                                    (o3_ref, l3_ref, so3_ref, sl3_ref, ATT_GROUPS[2][1])):
            n = rc // dil
            src = slice(kc * n, (kc + 1) * n)
            for r in range(dil):
                blk = o_ref[0, r, src, :].astype(F32)
                for h in range(HPG):
                    so_ref[h, pl.ds(kc * rc + r, n, stride=dil), :] = blk[:, h * DH:(h + 1) * DH]
                sl_ref[pl.ds(kc * rc + r, n, stride=dil), :] = l_ref[0, r, src, :]

        st = (l1_ref[0, 0, rows, :], sl2_ref[rows, :], sl3_ref[rows, :])
        mx = jnp.maximum(jnp.maximum(st[0], st[1]), st[2])
        wt = [jnp.exp2(t - mx) for t in st]
        for h in range(HPG):
            hs = slice(h * DH, (h + 1) * DH)
            cm = h * STAT_W
            cd = cm + STAT_W // 2
            wcol = [w[:, cm:cm + 1] for w in wt]
            den = wcol[0] * st[0][:, cd:cd + 1] + wcol[1] * st[1][:, cd:cd + 1] + wcol[2] * st[2][:, cd:cd + 1]
            inv = 1.0 / den
            oa = ((wcol[0] * inv) * o1_ref[0, 0, rows, hs].astype(F32)
                  + (wcol[1] * inv) * so2_ref[h, rows, :]
                  + (wcol[2] * inv) * so3_ref[h, rows, :])
            oa_ref[rows, hs] = oa.astype(BF16)

        ya = jnp.dot(oa_ref[rows, :], wa_ref[...], preferred_element_type=F32)
        yb = jnp.dot(or_ref[rows, :], wb_ref[...], preferred_element_type=F32)
        y = (ga_ref[rows, :].astype(F32) * ya + gb_ref[rows, :].astype(F32) * yb).astype(BF16)
        x1 = x_ref[rows, :] + jnp.dot(y, wo_ref[...], preferred_element_type=F32)
        xn2 = _rms_rows(x1, g2_ref[...]).astype(BF16)
        acc = x1
        for c in range(D_FF // ff_chunk):
            cs = slice(c * ff_chunk, (c + 1) * ff_chunk)
            hcol = jnp.maximum(jnp.dot(xn2, wu_ref[:, cs], preferred_element_type=F32), 0.0)
            acc = acc + jnp.dot((hcol * hcol).astype(BF16), wd_ref[cs, :], preferred_element_type=F32)
        out_ref[rows, :] = acc


def _post(x2, o_att, lse_att, o_r, r_mat, wa, wb, wo, g2, wu, wd, batch, seq, tm, rc, ff_chunk):
    T = x2.shape[0]
    nt = seq // tm
    gate_cb = (2 * RET_QK_W + 2 * RET_V_W) // D_MODEL

    def row(w, cb=0):
        return pl.BlockSpec((tm, w), lambda b, i: (b * nt + i, cb))

    def sub(w, gi):
        dil = ATT_GROUPS[gi][1]
        return pl.BlockSpec((1, dil, tm // dil, w), lambda b, i: (b, 0, i, 0))

    return pl.pallas_call(
        functools.partial(_post_kernel, tm=tm, rc=rc, ff_chunk=ff_chunk),
        grid=(batch, nt),
        in_specs=[row(D_MODEL), sub(GRP_W, 0), sub(GRP_W, 1), sub(GRP_W, 2), sub(DH, 0), sub(DH, 1), sub(DH, 2),
                  row(RET_V_W), row(D_MODEL, gate_cb), row(D_MODEL, gate_cb + 1),
                  _const_spec(wa.shape), _const_spec(wb.shape), _const_spec(wo.shape), _const_spec((1, D_MODEL)),
                  _const_spec(wu.shape), _const_spec(wd.shape)],
        out_specs=row(D_MODEL),
        out_shape=jax.ShapeDtypeStruct((T, D_MODEL), F32),
        scratch_shapes=[pltpu.VMEM((tm, GRP_W), BF16),
                        pltpu.VMEM((HPG, tm, DH), F32), pltpu.VMEM((HPG, tm, DH), F32),
                        pltpu.VMEM((tm, DH), F32), pltpu.VMEM((tm, DH), F32)],
        compiler_params=_cparams(("parallel", "parallel")),
        name="post",
    )(x2, *o_att, *lse_att, o_r, r_mat, r_mat, wa, wb, wo, g2, wu, wd)


def kernel(x, norm1_g, w_in, q_norm_g, k_norm_g, ret_gn_g, ret_gn_b, w_proj_a, w_proj_b, w_out, norm2_g, w_up, w_down):
    B, S, D = x.shape
    T = B * S
    depth = w_in.shape[0]
    x2 = x.reshape(T, D)
    for l in range(depth):
        w_l = w_in[l]
        g1 = norm1_g[l].reshape(1, D)
        qg = q_norm_g[l] * (DH ** -0.5 * LOG2E)
        kg = k_norm_g[l]

        o_att, lse_att = [], []
        for gi, (_, dil) in enumerate(ATT_GROUPS):
            cols = slice(gi * GRP_W, (gi + 1) * GRP_W)
            w_g = jnp.concatenate([w_l[:, p * ATT_W:(p + 1) * ATT_W][:, cols] for p in range(3)], axis=1).astype(BF16)
            hs = slice(gi * HPG, (gi + 1) * HPG)
            hg = jnp.concatenate([qg[hs].reshape(1, GRP_W), kg[hs].reshape(1, GRP_W)], axis=1)
            qkv = _proj_att(x2, g1, w_g, hg, dil, B, S, tm=1024, rc=256)
            o, lse = _attention(qkv, gi, tq=min(1024, S // dil), nr=max(1, 1024 * dil // S))
            o_att.append(o)
            lse_att.append(lse)

        r_mat = _proj_rest(x2, g1, w_l[:, 3 * ATT_W:].astype(BF16), tm=512, rc=256, tn=1024)
        o_r = _retention(r_mat, ret_gn_g[l].reshape(1, RET_V_W), ret_gn_b[l].reshape(1, RET_V_W), B, S, ts=1024, chunk=256)

        x2 = _post(x2, o_att, lse_att, o_r, r_mat,
                   w_proj_a[l].astype(BF16), w_proj_b[l].astype(BF16), w_out[l].astype(BF16),
                   norm2_g[l].reshape(1, D), w_up[l].astype(BF16), w_down[l].astype(BF16),
                   B, S, tm=512, rc=256, ff_chunk=1024)
    return x2.reshape(B, S, D)
```

```python
import functools

import numpy as np
import jax
import jax.numpy as jnp
from jax import lax
from jax.experimental import pallas as pl
from jax.experimental.pallas import tpu as pltpu

F32 = jnp.float32
BF16 = jnp.bfloat16

D_MODEL = 1024
ATT_GROUPS = ((128, 1), (512, 4), (2048, 16))
HPG = 4
ATT_HEADS = 12
DH = 128
LANE = 128
NSLAB = D_MODEL // LANE
ATT_BLOCK = 128
ATT_W = ATT_HEADS * DH
GRP_W = HPG * DH
STAT_W = LANE // HPG
RET_HEADS = 4
RET_DK = 256
RET_DV = 512
RET_QK_W = 1024
RET_V_W = 2048
D_FF = 4096
EPS = 1e-6
NEG = -1e30
LOG2E = float(np.log2(np.e))

VMEM_LIMIT = 56 * 1024 * 1024


def _cparams(sem):
    return pltpu.CompilerParams(dimension_semantics=sem, vmem_limit_bytes=VMEM_LIMIT)


def _const_spec(shape, index=None):
    nd = len(shape)
    index = (0,) * nd if index is None else index
    return pl.BlockSpec(shape, lambda *_: index, pipeline_mode=pl.Buffered(1))


def _nt_dot(a, b):
    return lax.dot_general(a, b, (((1,), (1,)), ((), ())), preferred_element_type=F32)


def _tn_dot(a, b):
    return lax.dot_general(a, b, (((0,), (0,)), ((), ())), preferred_element_type=F32)


def _rms_rows(x, g):
    return x * lax.rsqrt(jnp.mean(x * x, axis=-1, keepdims=True) + EPS) * g


def _sigmoid(t):
    return 1.0 / (1.0 + jnp.exp(-t))


def _att_kernel(*refs, dil, tm, rc):
    x_refs = refs[:NSLAB]
    (g1_ref, wq_ref, wk_ref, wv_ref, hg_ref, bias_ref, bias0_ref,
     o_ref, st_ref, qkv_ref, carry_ref) = refs[NSLAB:NSLAB + 11]
    n = tm // dil
    first = pl.program_id(1) == 0

    @pl.when(first)
    def _():
        carry_ref[...] = jnp.zeros_like(carry_ref)

    if dil == 16:
        xa_ref, xg_ref = refs[NSLAB + 11:]
        n4 = tm // 4
        for c in range(NSLAB):
            for lo in range(4):
                xa_ref[c, lo * n4:(lo + 1) * n4, :] = x_refs[c][pl.ds(lo, n4, stride=4), :]
        for c in range(NSLAB):
            for r in range(dil):
                xg_ref[c, r * n:(r + 1) * n, :] = xa_ref[c, pl.ds((r % 4) * n4 + r // 4, n, stride=4), :]

    def load_rows(c, r, l0, ln):
        if dil == 1:
            return x_refs[c][l0:l0 + ln, :]
        if dil == 16:
            return xg_ref[c, r * n + l0:r * n + l0 + ln, :]
        return x_refs[c][pl.ds(r + l0 * dil, ln, stride=dil), :]

    def project(kc):
        pieces, p = [], kc * rc
        while p < (kc + 1) * rc:
            r, l0 = divmod(p, n)
            ln = min(n - l0, (kc + 1) * rc - p)
            xs = [load_rows(c, r, l0, ln) for c in range(NSLAB)]
            ss = xs[0] * xs[0]
            for c in range(1, NSLAB):
                ss = ss + xs[c] * xs[c]
            scale = lax.rsqrt(jnp.sum(ss, axis=-1, keepdims=True) * (1.0 / D_MODEL) + EPS)
            pieces.append(jnp.concatenate(
                [(xs[c] * scale * g1_ref[:, c * LANE:(c + 1) * LANE]).astype(BF16) for c in range(NSLAB)], axis=1))
            p += ln
        xn = pieces[0] if len(pieces) == 1 else jnp.concatenate(pieces, axis=0)
        rows = slice(kc * rc, (kc + 1) * rc)
        for part, w_ref in enumerate((wq_ref, wk_ref)):
            acc = jnp.dot(xn, w_ref[...], preferred_element_type=F32)
            for h in range(HPG):
                hs = slice(h * DH, (h + 1) * DH)
                t = acc[:, hs]
                rr = lax.rsqrt(jnp.sum(t * t, axis=-1, keepdims=True) * (1.0 / DH) + EPS)
                gs = slice(part * GRP_W + h * DH, part * GRP_W + (h + 1) * DH)
                qkv_ref[rows, gs] = (t * rr * hg_ref[:, gs]).astype(BF16)
        qkv_ref[rows, 2 * GRP_W:] = jnp.dot(xn, wv_ref[...], preferred_element_type=F32).astype(BF16)

    lane = lax.broadcasted_iota(jnp.int32, (ATT_BLOCK, DH), 1)
    ones = jnp.ones((2 * ATT_BLOCK, DH), BF16)

    def attend(p0):
        r, l0 = divmod(p0, n)
        qrows = slice(p0, p0 + ATT_BLOCK)
        st_blk = None
        for h in range(HPG):
            hs = slice(h * DH, (h + 1) * DH)
            ks = slice(GRP_W + h * DH, GRP_W + (h + 1) * DH)
            vs = slice(2 * GRP_W + h * DH, 2 * GRP_W + (h + 1) * DH)
            q = qkv_ref[qrows, hs]
            if l0 == 0:
                k = jnp.concatenate([carry_ref[r, :, hs], qkv_ref[qrows, ks]], axis=0)
                v = jnp.concatenate([carry_ref[r, :, ks], qkv_ref[qrows, vs]], axis=0)
                b = jnp.where(first, bias0_ref[h], bias_ref[h])
            else:
                krows = slice(p0 - ATT_BLOCK, p0 + ATT_BLOCK)
                k = qkv_ref[krows, ks]
                v = qkv_ref[krows, vs]
                b = bias_ref[h]
            s = _nt_dot(q, k) + b
            m = jnp.max(s, axis=-1, keepdims=True)
            p = jnp.exp2(s - m).astype(BF16)
            oe = jnp.dot(p, jnp.concatenate([v, ones], axis=1), preferred_element_type=F32)
            o_ref[0, r, l0:l0 + ATT_BLOCK, hs] = oe[:, :DH].astype(BF16)
            lo = h * STAT_W
            mb = jnp.broadcast_to(m, (ATT_BLOCK, DH))
            st_blk = mb if h == 0 else jnp.where(lane >= lo, mb, st_blk)
            st_blk = jnp.where(lane >= lo + STAT_W // 2, oe[:, DH:], st_blk)
        st_ref[0, r, l0:l0 + ATT_BLOCK, :] = st_blk

    for kc in range(tm // rc):
        project(kc)
        for t in range(rc // ATT_BLOCK):
            attend(kc * rc + t * ATT_BLOCK)

    for r in range(dil):
        carry_ref[r] = qkv_ref[(r + 1) * n - ATT_BLOCK:(r + 1) * n, GRP_W:]


def _att_branch(x2, g1, w_bf, hg, gi, batch, seq, tm, rc):
    window, dil = ATT_GROUPS[gi]
    nt = seq // tm
    n = tm // dil

    slopes = 2.0 ** (-8.0 * np.arange(1, ATT_HEADS + 1, dtype=np.float32) / ATT_HEADS)
    slopes = slopes[gi * HPG:(gi + 1) * HPG]
    qi = np.arange(ATT_BLOCK)[:, None]
    kj = np.arange(2 * ATT_BLOCK)[None, :]
    dist = ATT_BLOCK + qi - kj
    valid = (dist >= 0) & (dist <= window // dil)
    alibi = -slopes[:, None, None] * (dist * dil).astype(np.float32)[None] * np.float32(LOG2E)
    bias = np.where(valid[None], alibi, NEG).astype(np.float32)
    bias0 = np.where((kj >= ATT_BLOCK)[None], bias, NEG).astype(np.float32)

    slab = [pl.BlockSpec((tm, LANE), functools.partial(lambda b, i, c: (b * nt + i, c), c=c)) for c in range(NSLAB)]
    ncb = ATT_W // GRP_W
    w_specs = [_const_spec((D_MODEL, GRP_W), (0, part * ncb + gi)) for part in range(3)]
    gather = [pltpu.VMEM((NSLAB, tm, LANE), F32)] * 2 if dil == 16 else []
    return pl.pallas_call(
        functools.partial(_att_kernel, dil=dil, tm=tm, rc=rc),
        grid=(batch, nt),
        in_specs=slab + [_const_spec((1, D_MODEL))] + w_specs
        + [_const_spec(hg.shape), _const_spec(bias.shape), _const_spec(bias0.shape)],
        out_specs=[pl.BlockSpec((1, dil, n, GRP_W), lambda b, i: (b, 0, i, 0)),
                   pl.BlockSpec((1, dil, n, DH), lambda b, i: (b, 0, i, 0))],
        out_shape=[jax.ShapeDtypeStruct((batch, dil, seq // dil, GRP_W), BF16),
                   jax.ShapeDtypeStruct((batch, dil, seq // dil, DH), F32)],
        scratch_shapes=[pltpu.VMEM((tm, 3 * GRP_W), BF16), pltpu.VMEM((dil, ATT_BLOCK, 2 * GRP_W), BF16)] + gather,
        compiler_params=_cparams(("parallel", "arbitrary")),
        name=f"att_d{dil}",
    )(*([x2] * NSLAB), g1, w_bf, w_bf, w_bf, hg, jnp.asarray(bias), jnp.asarray(bias0))


def _proj_rest_kernel(x_ref, g1_ref, w_ref, o_ref, *, tm, rc, tn):
    silu_lo = 2 * RET_QK_W + RET_V_W
    sig_lo = silu_lo + RET_V_W
    for kc in range(tm // rc):
        rows = slice(kc * rc, (kc + 1) * rc)
        xn = _rms_rows(x_ref[rows, :], g1_ref[...]).astype(BF16)
        for ct in range(o_ref.shape[1] // tn):
            cs = slice(ct * tn, (ct + 1) * tn)
            acc = jnp.dot(xn, w_ref[:, cs], preferred_element_type=F32)
            if ct * tn >= sig_lo:
                acc = _sigmoid(acc)
            elif ct * tn >= silu_lo:
                acc = acc * _sigmoid(acc)
            o_ref[rows, cs] = acc.astype(BF16)


def _proj_rest(x2, g1, w_rest, tm, rc, tn):
    T = x2.shape[0]
    n = w_rest.shape[1]
    return pl.pallas_call(
        functools.partial(_proj_rest_kernel, tm=tm, rc=rc, tn=tn),
        grid=(T // tm,),
        in_specs=[
            pl.BlockSpec((tm, D_MODEL), lambda i: (i, 0)),
            _const_spec((1, D_MODEL)),
            _const_spec(w_rest.shape),
        ],
        out_specs=pl.BlockSpec((tm, n), lambda i: (i, 0)),
        out_shape=jax.ShapeDtypeStruct((T, n), BF16),
        compiler_params=_cparams(("parallel",)),
        name="proj_rest",
    )(x2, g1, w_rest)


def _ret_kernel(q_ref, k_ref, v_ref, g_ref, gng_ref, gnb_ref, dec_ref, xi_ref, zeta_ref, gch_ref,
                o_ref, state_ref, *, chunk, nchunk):
    @pl.when(pl.program_id(2) == 0)
    def _():
        state_ref[...] = jnp.zeros_like(state_ref)

    for c in range(nchunk):
        rows = slice(c * chunk, (c + 1) * chunk)
        q = q_ref[rows, :]
        k = k_ref[rows, :]
        v = v_ref[rows, :]
        s = _nt_dot(q, k) * dec_ref[0]
        inner = jnp.dot(s.astype(BF16), v, preferred_element_type=F32)
        st = state_ref[...]
        cross = jnp.dot(q, st.astype(BF16), preferred_element_type=F32) * xi_ref[0]
        kz = (k.astype(F32) * zeta_ref[0]).astype(BF16)
        state_ref[...] = st * gch_ref[0] + _tn_dot(kz, v)
        o = inner + cross
        mu = jnp.mean(o, axis=-1, keepdims=True)
        oc = o - mu
        var = jnp.mean(oc * oc, axis=-1, keepdims=True)
        y = oc * lax.rsqrt(var + EPS) * gng_ref[...] + gnb_ref[...]
        o_ref[rows, :] = (y * g_ref[rows, :].astype(F32)).astype(BF16)


def _retention(r_mat, gn_g, gn_b, batch, seq, ts, chunk):
    T = r_mat.shape[0]
    ns = seq // ts
    H = RET_HEADS
    log_g = np.log(1.0 - 2.0 ** (-5.0 - np.arange(H, dtype=np.float64)))
    idx = np.arange(chunk, dtype=np.float64)
    diff = idx[:, None] - idx[None, :]
    scale = RET_DK ** -0.5
    dec = np.where(diff >= 0, np.exp(log_g[:, None, None] * np.maximum(diff, 0.0)), 0.0) * scale
    xi = np.exp(log_g[:, None] * (idx[None, :] + 1.0))
    zeta = np.exp(log_g[:, None] * (chunk - 1.0 - idx[None, :])) * scale
    gch = np.exp(log_g * chunk)
    dec = jnp.asarray(dec, F32)
    xi = jnp.asarray(np.broadcast_to(xi[:, :, None], (H, chunk, RET_DV)), F32)
    zeta = jnp.asarray(np.broadcast_to(zeta[:, :, None], (H, chunk, RET_DK)), F32)
    gch = jnp.asarray(np.broadcast_to(gch[:, None, None], (H, 1, RET_DV)), F32)

    return pl.pallas_call(
        functools.partial(_ret_kernel, chunk=chunk, nchunk=ts // chunk),
        grid=(batch, H, ns),
        in_specs=[
            pl.BlockSpec((ts, RET_DK), lambda b, h, i: (b * ns + i, h)),
            pl.BlockSpec((ts, RET_DK), lambda b, h, i: (b * ns + i, RET_QK_W // RET_DK + h)),
            pl.BlockSpec((ts, RET_DV), lambda b, h, i: (b * ns + i, 2 * RET_QK_W // RET_DV + h)),
            pl.BlockSpec((ts, RET_DV), lambda b, h, i: (b * ns + i, (2 * RET_QK_W + RET_V_W) // RET_DV + h)),
            pl.BlockSpec((1, RET_DV), lambda b, h, i: (0, h)),
            pl.BlockSpec((1, RET_DV), lambda b, h, i: (0, h)),
            pl.BlockSpec((1, chunk, chunk), lambda b, h, i: (h, 0, 0)),
            pl.BlockSpec((1, chunk, RET_DV), lambda b, h, i: (h, 0, 0)),
            pl.BlockSpec((1, chunk, RET_DK), lambda b, h, i: (h, 0, 0)),
            pl.BlockSpec((1, 1, RET_DV), lambda b, h, i: (h, 0, 0)),
        ],
        out_specs=pl.BlockSpec((ts, RET_DV), lambda b, h, i: (b * ns + i, h)),
        out_shape=jax.ShapeDtypeStruct((T, RET_V_W), BF16),
        scratch_shapes=[pltpu.VMEM((RET_DK, RET_DV), F32)],
        compiler_params=_cparams(("parallel", "parallel", "arbitrary")),
        name="retention",
    )(r_mat, r_mat, r_mat, r_mat, gn_g, gn_b, dec, xi, zeta, gch)


def _post_kernel(x_ref, o1_ref, o2_ref, o3_ref, l1_ref, l2_ref, l3_ref, or_ref, ga_ref, gb_ref,
                 wa_ref, wb_ref, wo_ref, g2_ref, wu_ref, wd_ref, out_ref,
                 oa_ref, so2_ref, so3_ref, sl2_ref, sl3_ref, *, tm, rc, ff_chunk):
    for kc in range(tm // rc):
        rows = slice(kc * rc, (kc + 1) * rc)
        for (o_ref, l_ref, so_ref, sl_ref, dil) in ((o2_ref, l2_ref, so2_ref, sl2_ref, ATT_GROUPS[1][1]),
                                                    (o3_ref, l3_ref, so3_ref, sl3_ref, ATT_GROUPS[2][1])):
            n = rc // dil
            src = slice(kc * n, (kc + 1) * n)
            for r in range(dil):
                blk = o_ref[0, r, src, :].astype(F32)
                for h in range(HPG):
                    so_ref[h, pl.ds(kc * rc + r, n, stride=dil), :] = blk[:, h * DH:(h + 1) * DH]
                sl_ref[pl.ds(kc * rc + r, n, stride=dil), :] = l_ref[0, r, src, :]

        st = (l1_ref[0, 0, rows, :], sl2_ref[rows, :], sl3_ref[rows, :])
        mx = jnp.maximum(jnp.maximum(st[0], st[1]), st[2])
        wt = [jnp.exp2(t - mx) for t in st]
        for h in range(HPG):
            hs = slice(h * DH, (h + 1) * DH)
            cm = h * STAT_W
            cd = cm + STAT_W // 2
            wcol = [w[:, cm:cm + 1] for w in wt]
            den = wcol[0] * st[0][:, cd:cd + 1] + wcol[1] * st[1][:, cd:cd + 1] + wcol[2] * st[2][:, cd:cd + 1]
            inv = 1.0 / den
            oa = ((wcol[0] * inv) * o1_ref[0, 0, rows, hs].astype(F32)
                  + (wcol[1] * inv) * so2_ref[h, rows, :]
                  + (wcol[2] * inv) * so3_ref[h, rows, :])
            oa_ref[rows, hs] = oa.astype(BF16)

        ya = jnp.dot(oa_ref[rows, :], wa_ref[...], preferred_element_type=F32)
        yb = jnp.dot(or_ref[rows, :], wb_ref[...], preferred_element_type=F32)
        y = (ga_ref[rows, :].astype(F32) * ya + gb_ref[rows, :].astype(F32) * yb).astype(BF16)
        x1 = x_ref[rows, :] + jnp.dot(y, wo_ref[...], preferred_element_type=F32)
        xn2 = _rms_rows(x1, g2_ref[...]).astype(BF16)
        acc = x1
        for c in range(D_FF // ff_chunk):
            cs = slice(c * ff_chunk, (c + 1) * ff_chunk)
            hcol = jnp.maximum(jnp.dot(xn2, wu_ref[:, cs], preferred_element_type=F32), 0.0)
            acc = acc + jnp.dot((hcol * hcol).astype(BF16), wd_ref[cs, :], preferred_element_type=F32)
        out_ref[rows, :] = acc


def _post(x2, o_att, st_att, o_r, r_mat, wa, wb, wo, g2, wu, wd, batch, seq, tm, rc, ff_chunk):
    T = x2.shape[0]
    nt = seq // tm
    gate_cb = (2 * RET_QK_W + 2 * RET_V_W) // D_MODEL

    def row(w, cb=0):
        return pl.BlockSpec((tm, w), lambda b, i: (b * nt + i, cb))

    def sub(w, gi):
        dil = ATT_GROUPS[gi][1]
        return pl.BlockSpec((1, dil, tm // dil, w), lambda b, i: (b, 0, i, 0))

    return pl.pallas_call(
        functools.partial(_post_kernel, tm=tm, rc=rc, ff_chunk=ff_chunk),
        grid=(batch, nt),
        in_specs=[row(D_MODEL), sub(GRP_W, 0), sub(GRP_W, 1), sub(GRP_W, 2), sub(DH, 0), sub(DH, 1), sub(DH, 2),
                  row(RET_V_W), row(D_MODEL, gate_cb), row(D_MODEL, gate_cb + 1),
                  _const_spec(wa.shape), _const_spec(wb.shape), _const_spec(wo.shape), _const_spec((1, D_MODEL)),
                  _const_spec(wu.shape), _const_spec(wd.shape)],
        out_specs=row(D_MODEL),
        out_shape=jax.ShapeDtypeStruct((T, D_MODEL), F32),
        scratch_shapes=[pltpu.VMEM((tm, GRP_W), BF16),
                        pltpu.VMEM((HPG, tm, DH), F32), pltpu.VMEM((HPG, tm, DH), F32),
                        pltpu.VMEM((tm, DH), F32), pltpu.VMEM((tm, DH), F32)],
        compiler_params=_cparams(("parallel", "parallel")),
        name="post",
    )(x2, *o_att, *st_att, o_r, r_mat, r_mat, wa, wb, wo, g2, wu, wd)


def kernel(x, norm1_g, w_in, q_norm_g, k_norm_g, ret_gn_g, ret_gn_b, w_proj_a, w_proj_b, w_out, norm2_g, w_up, w_down):
    B, S, D = x.shape
    T = B * S
    depth = w_in.shape[0]
    x2 = x.reshape(T, D)
    for l in range(depth):
        w_bf = w_in[l].astype(BF16)
        g1 = norm1_g[l].reshape(1, D)
        qg = q_norm_g[l] * (DH ** -0.5 * LOG2E)
        kg = k_norm_g[l]

        o_att, st_att = [], []
        for gi, (_, dil) in enumerate(ATT_GROUPS):
            hs = slice(gi * HPG, (gi + 1) * HPG)
            hg = jnp.concatenate([qg[hs].reshape(1, GRP_W), kg[hs].reshape(1, GRP_W)], axis=1)
            o, st = _att_branch(x2, g1, w_bf, hg, gi, B, S, tm=max(1024, ATT_BLOCK * dil), rc=256)
            o_att.append(o)
            st_att.append(st)

        r_mat = _proj_rest(x2, g1, w_bf[:, 3 * ATT_W:], tm=512, rc=256, tn=1024)
        o_r = _retention(r_mat, ret_gn_g[l].reshape(1, RET_V_W), ret_gn_b[l].reshape(1, RET_V_W), B, S, ts=1024, chunk=256)

        x2 = _post(x2, o_att, st_att, o_r, r_mat,
                   w_proj_a[l].astype(BF16), w_proj_b[l].astype(BF16), w_out[l].astype(BF16),
                   norm2_g[l].reshape(1, D), w_up[l].astype(BF16), w_down[l].astype(BF16),
                   B, S, tm=512, rc=256, ff_chunk=1024)
    return x2.reshape(B, S, D)
```

```python
import functools

import numpy as np
import jax
import jax.numpy as jnp
from jax import lax
from jax.experimental import pallas as pl
from jax.experimental.pallas import tpu as pltpu

F32 = jnp.float32
BF16 = jnp.bfloat16

D_MODEL = 1024
ATT_GROUPS = ((128, 1), (512, 4), (2048, 16))
HPG = 4
ATT_HEADS = 12
DH = 128
LANE = 128
NSLAB = D_MODEL // LANE
ATT_BLOCK = 128
ATT_W = ATT_HEADS * DH
GRP_W = HPG * DH
STAT_W = LANE // HPG
RET_HEADS = 4
RET_DK = 256
RET_DV = 512
RET_QK_W = 1024
RET_V_W = 2048
D_FF = 4096
EPS = 1e-6
NEG = -1e30
LOG2E = float(np.log2(np.e))

VMEM_LIMIT = 56 * 1024 * 1024


def _cparams(sem):
    return pltpu.CompilerParams(dimension_semantics=sem, vmem_limit_bytes=VMEM_LIMIT)


def _const_spec(shape, index=None):
    nd = len(shape)
    index = (0,) * nd if index is None else index
    return pl.BlockSpec(shape, lambda *_: index, pipeline_mode=pl.Buffered(1))


def _nt_dot(a, b):
    return lax.dot_general(a, b, (((1,), (1,)), ((), ())), preferred_element_type=F32)


def _tn_dot(a, b):
    return lax.dot_general(a, b, (((0,), (0,)), ((), ())), preferred_element_type=F32)


def _rms_rows(x, g):
    return x * lax.rsqrt(jnp.mean(x * x, axis=-1, keepdims=True) + EPS) * g


def _sigmoid(t):
    return 1.0 / (1.0 + jnp.exp(-t))


def _att_kernel(*refs, dil, tm, rc):
    x_refs = refs[:NSLAB]
    (g1_ref, wq_ref, wk_ref, wv_ref, hg_ref, bias_ref, bias0_ref,
     o_ref, st_ref, qkv_ref, carry_ref) = refs[NSLAB:NSLAB + 11]
    n = tm // dil
    first = pl.program_id(1) == 0

    @pl.when(first)
    def _():
        carry_ref[...] = jnp.zeros_like(carry_ref)

    if dil == 16:
        xa_ref, xg_ref = refs[NSLAB + 11:]
        n4 = tm // 4
        for c in range(NSLAB):
            for lo in range(4):
                xa_ref[c, lo * n4:(lo + 1) * n4, :] = x_refs[c][pl.ds(lo, n4, stride=4), :]
        for c in range(NSLAB):
            for r in range(dil):
                xg_ref[c, r * n:(r + 1) * n, :] = xa_ref[c, pl.ds((r % 4) * n4 + r // 4, n, stride=4), :]

    def load_rows(c, r, l0, ln):
        if dil == 1:
            return x_refs[c][l0:l0 + ln, :]
        if dil == 16:
            return xg_ref[c, r * n + l0:r * n + l0 + ln, :]
        return x_refs[c][pl.ds(r + l0 * dil, ln, stride=dil), :]

    def project(kc):
        pieces, p = [], kc * rc
        while p < (kc + 1) * rc:
            r, l0 = divmod(p, n)
            ln = min(n - l0, (kc + 1) * rc - p)
            xs = [load_rows(c, r, l0, ln) for c in range(NSLAB)]
            ss = xs[0] * xs[0]
            for c in range(1, NSLAB):
                ss = ss + xs[c] * xs[c]
            scale = lax.rsqrt(jnp.sum(ss, axis=-1, keepdims=True) * (1.0 / D_MODEL) + EPS)
            pieces.append(jnp.concatenate(
                [(xs[c] * scale * g1_ref[:, c * LANE:(c + 1) * LANE]).astype(BF16) for c in range(NSLAB)], axis=1))
            p += ln
        xn = pieces[0] if len(pieces) == 1 else jnp.concatenate(pieces, axis=0)
        rows = slice(kc * rc, (kc + 1) * rc)
        for part, w_ref in enumerate((wq_ref, wk_ref)):
            acc = jnp.dot(xn, w_ref[...], preferred_element_type=F32)
            for h in range(HPG):
                hs = slice(h * DH, (h + 1) * DH)
                t = acc[:, hs]
                rr = lax.rsqrt(jnp.sum(t * t, axis=-1, keepdims=True) * (1.0 / DH) + EPS)
                gs = slice(part * GRP_W + h * DH, part * GRP_W + (h + 1) * DH)
                qkv_ref[rows, gs] = (t * rr * hg_ref[:, gs]).astype(BF16)
        qkv_ref[rows, 2 * GRP_W:] = jnp.dot(xn, wv_ref[...], preferred_element_type=F32).astype(BF16)

    lane = lax.broadcasted_iota(jnp.int32, (ATT_BLOCK, DH), 1)
    ones = jnp.ones((2 * ATT_BLOCK, DH), BF16)

    def attend(p0):
        r, l0 = divmod(p0, n)
        qrows = slice(p0, p0 + ATT_BLOCK)
        st_blk = None
        for h in range(HPG):
            hs = slice(h * DH, (h + 1) * DH)
            ks = slice(GRP_W + h * DH, GRP_W + (h + 1) * DH)
            vs = slice(2 * GRP_W + h * DH, 2 * GRP_W + (h + 1) * DH)
            q = qkv_ref[qrows, hs]
            if l0 == 0:
                k = jnp.concatenate([carry_ref[r, :, hs], qkv_ref[qrows, ks]], axis=0)
                v = jnp.concatenate([carry_ref[r, :, ks], qkv_ref[qrows, vs]], axis=0)
                b = jnp.where(first, bias0_ref[h], bias_ref[h])
            else:
                krows = slice(p0 - ATT_BLOCK, p0 + ATT_BLOCK)
                k = qkv_ref[krows, ks]
                v = qkv_ref[krows, vs]
                b = bias_ref[h]
            s = _nt_dot(q, k) + b
            m = jnp.max(s, axis=-1, keepdims=True)
            p = jnp.exp2(s - m).astype(BF16)
            oe = jnp.dot(p, jnp.concatenate([v, ones], axis=1), preferred_element_type=F32)
            o_ref[0, r, l0:l0 + ATT_BLOCK, hs] = oe[:, :DH].astype(BF16)
            lo = h * STAT_W
            mb = jnp.broadcast_to(m, (ATT_BLOCK, DH))
            st_blk = mb if h == 0 else jnp.where(lane >= lo, mb, st_blk)
            st_blk = jnp.where(lane >= lo + STAT_W // 2, oe[:, DH:], st_blk)
        st_ref[0, r, l0:l0 + ATT_BLOCK, :] = st_blk

    for kc in range(tm // rc):
        project(kc)
        for t in range(rc // ATT_BLOCK):
            attend(kc * rc + t * ATT_BLOCK)

    for r in range(dil):
        carry_ref[r] = qkv_ref[(r + 1) * n - ATT_BLOCK:(r + 1) * n, GRP_W:]


def _att_branch(x2, g1, w_bf, hg, gi, batch, seq, tm, rc):
    window, dil = ATT_GROUPS[gi]
    nt = seq // tm
    n = tm // dil

    slopes = 2.0 ** (-8.0 * np.arange(1, ATT_HEADS + 1, dtype=np.float32) / ATT_HEADS)
    slopes = slopes[gi * HPG:(gi + 1) * HPG]
    qi = np.arange(ATT_BLOCK)[:, None]
    kj = np.arange(2 * ATT_BLOCK)[None, :]
    dist = ATT_BLOCK + qi - kj
    valid = (dist >= 0) & (dist <= window // dil)
    alibi = -slopes[:, None, None] * (dist * dil).astype(np.float32)[None] * np.float32(LOG2E)
    bias = np.where(valid[None], alibi, NEG).astype(np.float32)
    bias0 = np.where((kj >= ATT_BLOCK)[None], bias, NEG).astype(np.float32)

    slab = [pl.BlockSpec((tm, LANE), functools.partial(lambda b, i, c: (b * nt + i, c), c=c)) for c in range(NSLAB)]
    ncb = ATT_W // GRP_W
    w_specs = [_const_spec((D_MODEL, GRP_W), (0, part * ncb + gi)) for part in range(3)]
    gather = [pltpu.VMEM((NSLAB, tm, LANE), F32)] * 2 if dil == 16 else []
    return pl.pallas_call(
        functools.partial(_att_kernel, dil=dil, tm=tm, rc=rc),
        grid=(batch, nt),
        in_specs=slab + [_const_spec((1, D_MODEL))] + w_specs
        + [_const_spec(hg.shape), _const_spec(bias.shape), _const_spec(bias0.shape)],
        out_specs=[pl.BlockSpec((1, dil, n, GRP_W), lambda b, i: (b, 0, i, 0)),
                   pl.BlockSpec((1, dil, n, DH), lambda b, i: (b, 0, i, 0))],
        out_shape=[jax.ShapeDtypeStruct((batch, dil, seq // dil, GRP_W), BF16),
                   jax.ShapeDtypeStruct((batch, dil, seq // dil, DH), F32)],
        scratch_shapes=[pltpu.VMEM((tm, 3 * GRP_W), BF16), pltpu.VMEM((dil, ATT_BLOCK, 2 * GRP_W), BF16)] + gather,
        compiler_params=_cparams(("parallel", "arbitrary")),
        name=f"att_d{dil}",
    )(*([x2] * NSLAB), g1, w_bf, w_bf, w_bf, hg, jnp.asarray(bias), jnp.asarray(bias0))


def _proj_rest_kernel(x_ref, g1_ref, w_ref, o_ref, *, tm, rc, tn):
    silu_lo = 2 * RET_QK_W + RET_V_W
    sig_lo = silu_lo + RET_V_W
    for kc in range(tm // rc):
        rows = slice(kc * rc, (kc + 1) * rc)
        xn = _rms_rows(x_ref[rows, :], g1_ref[...]).astype(BF16)
        for ct in range(o_ref.shape[1] // tn):
            cs = slice(ct * tn, (ct + 1) * tn)
            acc = jnp.dot(xn, w_ref[:, 3 * ATT_W + ct * tn:3 * ATT_W + (ct + 1) * tn], preferred_element_type=F32)
            if ct * tn >= sig_lo:
                acc = _sigmoid(acc)
            elif ct * tn >= silu_lo:
                acc = acc * _sigmoid(acc)
            o_ref[rows, cs] = acc.astype(BF16)


def _proj_rest(x2, g1, w_bf, tm, rc, tn):
    T = x2.shape[0]
    n = w_bf.shape[1] - 3 * ATT_W
    return pl.pallas_call(
        functools.partial(_proj_rest_kernel, tm=tm, rc=rc, tn=tn),
        grid=(T // tm,),
        in_specs=[
            pl.BlockSpec((tm, D_MODEL), lambda i: (i, 0)),
            _const_spec((1, D_MODEL)),
            _const_spec(w_bf.shape),
        ],
        out_specs=pl.BlockSpec((tm, n), lambda i: (i, 0)),
        out_shape=jax.ShapeDtypeStruct((T, n), BF16),
        compiler_params=_cparams(("parallel",)),
        name="proj_rest",
    )(x2, g1, w_bf)


def _retention_tables(chunk):
    H = RET_HEADS
    log_g = np.log(1.0 - 2.0 ** (-5.0 - np.arange(H, dtype=np.float64)))
    idx = np.arange(chunk, dtype=np.float64)
    diff = idx[:, None] - idx[None, :]
    scale = RET_DK ** -0.5
    dec = np.where(diff >= 0, np.exp(log_g[:, None, None] * np.maximum(diff, 0.0)), 0.0) * scale
    xi = np.exp(log_g[:, None] * (idx[None, :] + 1.0))
    zeta = np.exp(log_g[:, None] * (chunk - 1.0 - idx[None, :])) * scale
    gch = np.exp(log_g * chunk)
    return (jnp.asarray(dec, F32),
            jnp.asarray(np.broadcast_to(xi[:, :, None], (H, chunk, RET_DV)), F32),
            jnp.asarray(np.broadcast_to(zeta[:, :, None], (H, chunk, RET_DK)), F32),
            jnp.asarray(np.broadcast_to(gch[:, None, None], (H, 1, RET_DV)), F32))


def _retention_chunk(q, k, v, sg, state_ref, h, gng, gnb, dec_ref, xi_ref, zeta_ref, gch_ref):
    s = _nt_dot(q, k) * dec_ref[h]
    inner = jnp.dot(s.astype(BF16), v, preferred_element_type=F32)
    st = state_ref[h]
    cross = jnp.dot(q, st.astype(BF16), preferred_element_type=F32) * xi_ref[h]
    kz = (k.astype(F32) * zeta_ref[h]).astype(BF16)
    state_ref[h] = st * gch_ref[h] + _tn_dot(kz, v)
    o = inner + cross
    mu = jnp.mean(o, axis=-1, keepdims=True)
    oc = o - mu
    var = jnp.mean(oc * oc, axis=-1, keepdims=True)
    return (oc * lax.rsqrt(var + EPS) * gng + gnb) * sg.astype(F32)


def _mix_kernel(x_ref, o1_ref, o2_ref, o3_ref, l1_ref, l2_ref, l3_ref, r_ref,
                wa_ref, wb_ref, wo_ref, gng_ref, gnb_ref, dec_ref, xi_ref, zeta_ref, gch_ref, out_ref,
                oa_ref, or_ref, so2_ref, so3_ref, sl2_ref, sl3_ref, state_ref, *, tm, rc):
    @pl.when(pl.program_id(1) == 0)
    def _():
        state_ref[...] = jnp.zeros_like(state_ref)

    v_lo = 2 * RET_QK_W
    sg_lo = v_lo + RET_V_W
    ga_lo = sg_lo + RET_V_W
    gb_lo = ga_lo + D_MODEL
    for kc in range(tm // rc):
        rows = slice(kc * rc, (kc + 1) * rc)
        for (o_ref, l_ref, so_ref, sl_ref, dil) in ((o2_ref, l2_ref, so2_ref, sl2_ref, ATT_GROUPS[1][1]),
                                                    (o3_ref, l3_ref, so3_ref, sl3_ref, ATT_GROUPS[2][1])):
            n = rc // dil
            src = slice(kc * n, (kc + 1) * n)
            for r in range(dil):
                blk = o_ref[0, r, src, :].astype(F32)
                for h in range(HPG):
                    so_ref[h, pl.ds(kc * rc + r, n, stride=dil), :] = blk[:, h * DH:(h + 1) * DH]
                sl_ref[pl.ds(kc * rc + r, n, stride=dil), :] = l_ref[0, r, src, :]

        st = (l1_ref[0, 0, rows, :], sl2_ref[rows, :], sl3_ref[rows, :])
        mx = jnp.maximum(jnp.maximum(st[0], st[1]), st[2])
        wt = [jnp.exp2(t - mx) for t in st]
        for h in range(HPG):
            hs = slice(h * DH, (h + 1) * DH)
            cm = h * STAT_W
            cd = cm + STAT_W // 2
            wcol = [w[:, cm:cm + 1] for w in wt]
            den = wcol[0] * st[0][:, cd:cd + 1] + wcol[1] * st[1][:, cd:cd + 1] + wcol[2] * st[2][:, cd:cd + 1]
            inv = 1.0 / den
            oa = ((wcol[0] * inv) * o1_ref[0, 0, rows, hs].astype(F32)
                  + (wcol[1] * inv) * so2_ref[h, rows, :]
                  + (wcol[2] * inv) * so3_ref[h, rows, :])
            oa_ref[rows, hs] = oa.astype(BF16)

        for h in range(RET_HEADS):
            qs = slice(h * RET_DK, (h + 1) * RET_DK)
            ks = slice(RET_QK_W + h * RET_DK, RET_QK_W + (h + 1) * RET_DK)
            vs = slice(v_lo + h * RET_DV, v_lo + (h + 1) * RET_DV)
            gs = slice(sg_lo + h * RET_DV, sg_lo + (h + 1) * RET_DV)
            ns = slice(h * RET_DV, (h + 1) * RET_DV)
            y_h = _retention_chunk(r_ref[rows, qs], r_ref[rows, ks], r_ref[rows, vs], r_ref[rows, gs], state_ref, h,
                                   gng_ref[:, ns], gnb_ref[:, ns], dec_ref, xi_ref, zeta_ref, gch_ref)
            or_ref[rows, ns] = y_h.astype(BF16)

        ya = jnp.dot(oa_ref[rows, :], wa_ref[...], preferred_element_type=F32)
        yb = jnp.dot(or_ref[rows, :], wb_ref[...], preferred_element_type=F32)
        ga = r_ref[rows, ga_lo:gb_lo].astype(F32)
        gb = r_ref[rows, gb_lo:].astype(F32)
        y = (ga * ya + gb * yb).astype(BF16)
        out_ref[rows, :] = x_ref[rows, :] + jnp.dot(y, wo_ref[...], preferred_element_type=F32)


def _mix(x2, o_att, st_att, r_mat, wa, wb, wo, gn_g, gn_b, batch, seq, tm, rc):
    T = x2.shape[0]
    nt = seq // tm
    tables = _retention_tables(rc)

    def row(w):
        return pl.BlockSpec((tm, w), lambda b, i: (b * nt + i, 0))

    def sub(w, gi):
        dil = ATT_GROUPS[gi][1]
        return pl.BlockSpec((1, dil, tm // dil, w), lambda b, i: (b, 0, i, 0))

    return pl.pallas_call(
        functools.partial(_mix_kernel, tm=tm, rc=rc),
        grid=(batch, nt),
        in_specs=[row(D_MODEL), sub(GRP_W, 0), sub(GRP_W, 1), sub(GRP_W, 2), sub(DH, 0), sub(DH, 1), sub(DH, 2),
                  row(r_mat.shape[1]),
                  _const_spec(wa.shape), _const_spec(wb.shape), _const_spec(wo.shape),
                  _const_spec(gn_g.shape), _const_spec(gn_b.shape)] + [_const_spec(t.shape) for t in tables],
        out_specs=row(D_MODEL),
        out_shape=jax.ShapeDtypeStruct((T, D_MODEL), F32),
        scratch_shapes=[pltpu.VMEM((tm, GRP_W), BF16), pltpu.VMEM((tm, RET_V_W), BF16),
                        pltpu.VMEM((HPG, tm, DH), F32), pltpu.VMEM((HPG, tm, DH), F32),
                        pltpu.VMEM((tm, DH), F32), pltpu.VMEM((tm, DH), F32),
                        pltpu.VMEM((RET_HEADS, RET_DK, RET_DV), F32)],
        compiler_params=_cparams(("parallel", "arbitrary")),
        name="mix",
    )(x2, *o_att, *st_att, r_mat, wa, wb, wo, gn_g, gn_b, *tables)


def _mlp_kernel(x_ref, g2_ref, wu_ref, wd_ref, out_ref, *, tm, rc, ff_chunk):
    for kc in range(tm // rc):
        rows = slice(kc * rc, (kc + 1) * rc)
        x1 = x_ref[rows, :]
        xn2 = _rms_rows(x1, g2_ref[...]).astype(BF16)
        acc = x1
        for c in range(D_FF // ff_chunk):
            cs = slice(c * ff_chunk, (c + 1) * ff_chunk)
            hcol = jnp.maximum(jnp.dot(xn2, wu_ref[:, cs], preferred_element_type=F32), 0.0)
            acc = acc + jnp.dot((hcol * hcol).astype(BF16), wd_ref[cs, :], preferred_element_type=F32)
        out_ref[rows, :] = acc


def _mlp(x1, g2, wu, wd, tm, rc, ff_chunk):
    T = x1.shape[0]
    return pl.pallas_call(
        functools.partial(_mlp_kernel, tm=tm, rc=rc, ff_chunk=ff_chunk),
        grid=(T // tm,),
        in_specs=[pl.BlockSpec((tm, D_MODEL), lambda i: (i, 0)), _const_spec((1, D_MODEL)),
                  _const_spec(wu.shape), _const_spec(wd.shape)],
        out_specs=pl.BlockSpec((tm, D_MODEL), lambda i: (i, 0)),
        out_shape=jax.ShapeDtypeStruct((T, D_MODEL), F32),
        compiler_params=_cparams(("parallel",)),
        name="mlp",
    )(x1, g2, wu, wd)


def kernel(x, norm1_g, w_in, q_norm_g, k_norm_g, ret_gn_g, ret_gn_b, w_proj_a, w_proj_b, w_out, norm2_g, w_up, w_down):
    B, S, D = x.shape
    T = B * S
    depth = w_in.shape[0]
    x2 = x.reshape(T, D)
    for l in range(depth):
        w_bf = w_in[l].astype(BF16)
        g1 = norm1_g[l].reshape(1, D)
        qg = q_norm_g[l] * (DH ** -0.5 * LOG2E)
        kg = k_norm_g[l]

        o_att, st_att = [], []
        for gi, (_, dil) in enumerate(ATT_GROUPS):
            hs = slice(gi * HPG, (gi + 1) * HPG)
            hg = jnp.concatenate([qg[hs].reshape(1, GRP_W), kg[hs].reshape(1, GRP_W)], axis=1)
            o, st = _att_branch(x2, g1, w_bf, hg, gi, B, S, tm=max(1024, ATT_BLOCK * dil), rc=256)
            o_att.append(o)
            st_att.append(st)

        r_mat = _proj_rest(x2, g1, w_bf, tm=512, rc=256, tn=1024)

        x1 = _mix(x2, o_att, st_att, r_mat,
                  w_proj_a[l].astype(BF16), w_proj_b[l].astype(BF16), w_out[l].astype(BF16),
                  ret_gn_g[l].reshape(1, RET_V_W), ret_gn_b[l].reshape(1, RET_V_W), B, S, tm=512, rc=256)
        x2 = _mlp(x1, norm2_g[l].reshape(1, D), w_up[l].astype(BF16), w_down[l].astype(BF16),
                  tm=1024, rc=256, ff_chunk=1024)
    return x2.reshape(B, S, D)
```

```python
import functools

import numpy as np
import jax
import jax.numpy as jnp
from jax import lax
from jax.experimental import pallas as pl
from jax.experimental.pallas import tpu as pltpu

F32 = jnp.float32
BF16 = jnp.bfloat16

D_MODEL = 1024
ATT_GROUPS = ((128, 1), (512, 4), (2048, 16))
HPG = 4
ATT_HEADS = 12
DH = 128
LANE = 128
NSLAB = D_MODEL // LANE
ATT_BLOCK = 128
ATT_W = ATT_HEADS * DH
GRP_W = HPG * DH
STAT_W = LANE // HPG
RET_HEADS = 4
RET_DK = 256
RET_DV = 512
RET_QK_W = 1024
RET_V_W = 2048
D_FF = 4096
EPS = 1e-6
NEG = -1e30
LOG2E = float(np.log2(np.e))

VMEM_LIMIT = 56 * 1024 * 1024


def _cparams(sem):
    return pltpu.CompilerParams(dimension_semantics=sem, vmem_limit_bytes=VMEM_LIMIT)


def _const_spec(shape, index=None):
    nd = len(shape)
    index = (0,) * nd if index is None else index
    return pl.BlockSpec(shape, lambda *_: index, pipeline_mode=pl.Buffered(1))


def _nt_dot(a, b):
    return lax.dot_general(a, b, (((1,), (1,)), ((), ())), preferred_element_type=F32)


def _tn_dot(a, b):
    return lax.dot_general(a, b, (((0,), (0,)), ((), ())), preferred_element_type=F32)


def _rms_rows(x, g):
    return x * lax.rsqrt(jnp.mean(x * x, axis=-1, keepdims=True) + EPS) * g


def _sigmoid(t):
    return 1.0 / (1.0 + jnp.exp(-t))


def _num_x_refs(dil):
    return 1 if dil == 1 else NSLAB


def _att_kernel(*refs, dil, tm, rc):
    nx = _num_x_refs(dil)
    x_refs = refs[:nx]
    (g1_ref, wq_ref, wk_ref, wv_ref, hg_ref, bias_ref, bias0_ref,
     o_ref, st_ref, qkv_ref, carry_ref) = refs[nx:nx + 11]
    n = tm // dil
    first = pl.program_id(1) == 0

    @pl.when(first)
    def _():
        carry_ref[...] = jnp.zeros_like(carry_ref)

    if dil == 16:
        xa_ref, = refs[nx + 11:]
        n4 = tm // 4
        for c in range(NSLAB):
            for lo in range(4):
                xa_ref[c, lo * n4:(lo + 1) * n4, :] = x_refs[c][pl.ds(lo, n4, stride=4), :]

    def load_rows(c, r, l0, ln):
        if dil == 1:
            return x_refs[0][l0:l0 + ln, c * LANE:(c + 1) * LANE]
        if dil == 16:
            return xa_ref[c, pl.ds((r % 4) * n4 + r // 4 + 4 * l0, ln, stride=4), :]
        return x_refs[c][pl.ds(r + l0 * dil, ln, stride=dil), :]

    def project(kc):
        pieces, p = [], kc * rc
        while p < (kc + 1) * rc:
            r, l0 = divmod(p, n)
            ln = min(n - l0, (kc + 1) * rc - p)
            xs = [load_rows(c, r, l0, ln) for c in range(NSLAB)]
            ss = xs[0] * xs[0]
            for c in range(1, NSLAB):
                ss = ss + xs[c] * xs[c]
            scale = lax.rsqrt(jnp.sum(ss, axis=-1, keepdims=True) * (1.0 / D_MODEL) + EPS)
            pieces.append(jnp.concatenate(
                [(xs[c] * scale * g1_ref[:, c * LANE:(c + 1) * LANE]).astype(BF16) for c in range(NSLAB)], axis=1))
            p += ln
        xn = pieces[0] if len(pieces) == 1 else jnp.concatenate(pieces, axis=0)
        rows = slice(kc * rc, (kc + 1) * rc)
        for part, w_ref in enumerate((wq_ref, wk_ref)):
            acc = jnp.dot(xn, w_ref[...], preferred_element_type=F32)
            for h in range(HPG):
                hs = slice(h * DH, (h + 1) * DH)
                t = acc[:, hs]
                rr = lax.rsqrt(jnp.sum(t * t, axis=-1, keepdims=True) * (1.0 / DH) + EPS)
                gs = slice(part * GRP_W + h * DH, part * GRP_W + (h + 1) * DH)
                qkv_ref[rows, gs] = (t * rr * hg_ref[:, gs]).astype(BF16)
        qkv_ref[rows, 2 * GRP_W:] = jnp.dot(xn, wv_ref[...], preferred_element_type=F32).astype(BF16)

    lane = lax.broadcasted_iota(jnp.int32, (ATT_BLOCK, DH), 1)
    ones = jnp.ones((2 * ATT_BLOCK, DH), BF16)

    def attend(p0):
        r, l0 = divmod(p0, n)
        qrows = slice(p0, p0 + ATT_BLOCK)
        st_blk = None
        for h in range(HPG):
            hs = slice(h * DH, (h + 1) * DH)
            ks = slice(GRP_W + h * DH, GRP_W + (h + 1) * DH)
            vs = slice(2 * GRP_W + h * DH, 2 * GRP_W + (h + 1) * DH)
            q = qkv_ref[qrows, hs]
            if l0 == 0:
                k = jnp.concatenate([carry_ref[r, :, hs], qkv_ref[qrows, ks]], axis=0)
                v = jnp.concatenate([carry_ref[r, :, ks], qkv_ref[qrows, vs]], axis=0)
                b = jnp.where(first, bias0_ref[h], bias_ref[h])
            else:
                krows = slice(p0 - ATT_BLOCK, p0 + ATT_BLOCK)
                k = qkv_ref[krows, ks]
                v = qkv_ref[krows, vs]
                b = bias_ref[h]
            s = _nt_dot(q, k) + b
            m = jnp.max(s, axis=-1, keepdims=True)
            p = jnp.exp2(s - m).astype(BF16)
            oe = jnp.dot(p, jnp.concatenate([v, ones], axis=1), preferred_element_type=F32)
            o_ref[0, r, l0:l0 + ATT_BLOCK, hs] = oe[:, :DH].astype(BF16)
            lo = h * STAT_W
            mb = jnp.broadcast_to(m, (ATT_BLOCK, DH))
            st_blk = mb if h == 0 else jnp.where(lane >= lo, mb, st_blk)
            st_blk = jnp.where(lane >= lo + STAT_W // 2, oe[:, DH:], st_blk)
        st_ref[0, r, l0:l0 + ATT_BLOCK, :] = st_blk

    for kc in range(tm // rc):
        project(kc)
    for t in range(tm // ATT_BLOCK):
        attend(t * ATT_BLOCK)

    for r in range(dil):
        carry_ref[r] = qkv_ref[(r + 1) * n - ATT_BLOCK:(r + 1) * n, GRP_W:]


def _att_branch(x2, g1, w_bf, hg, gi, batch, seq, tm, rc):
    window, dil = ATT_GROUPS[gi]
    nt = seq // tm
    n = tm // dil

    slopes = 2.0 ** (-8.0 * np.arange(1, ATT_HEADS + 1, dtype=np.float32) / ATT_HEADS)
    slopes = slopes[gi * HPG:(gi + 1) * HPG]
    qi = np.arange(ATT_BLOCK)[:, None]
    kj = np.arange(2 * ATT_BLOCK)[None, :]
    dist = ATT_BLOCK + qi - kj
    valid = (dist >= 0) & (dist <= window // dil)
    alibi = -slopes[:, None, None] * (dist * dil).astype(np.float32)[None] * np.float32(LOG2E)
    bias = np.where(valid[None], alibi, NEG).astype(np.float32)
    bias0 = np.where((kj >= ATT_BLOCK)[None], bias, NEG).astype(np.float32)

    nx = _num_x_refs(dil)
    slab = [pl.BlockSpec((tm, D_MODEL // nx), functools.partial(lambda b, i, c: (b * nt + i, c), c=c))
            for c in range(nx)]
    ncb = ATT_W // GRP_W
    w_specs = [_const_spec((D_MODEL, GRP_W), (0, part * ncb + gi)) for part in range(3)]
    gather = [pltpu.VMEM((NSLAB, tm, LANE), F32)] if dil == 16 else []
    return pl.pallas_call(
        functools.partial(_att_kernel, dil=dil, tm=tm, rc=rc),
        grid=(batch, nt),
        in_specs=slab + [_const_spec((1, D_MODEL))] + w_specs
        + [_const_spec(hg.shape), _const_spec(bias.shape), _const_spec(bias0.shape)],
        out_specs=[pl.BlockSpec((1, dil, n, GRP_W), lambda b, i: (b, 0, i, 0)),
                   pl.BlockSpec((1, dil, n, DH), lambda b, i: (b, 0, i, 0))],
        out_shape=[jax.ShapeDtypeStruct((batch, dil, seq // dil, GRP_W), BF16),
                   jax.ShapeDtypeStruct((batch, dil, seq // dil, DH), F32)],
        scratch_shapes=[pltpu.VMEM((tm, 3 * GRP_W), BF16), pltpu.VMEM((dil, ATT_BLOCK, 2 * GRP_W), BF16)] + gather,
        compiler_params=_cparams(("parallel", "arbitrary")),
        name=f"att_d{dil}",
    )(*([x2] * nx), g1, w_bf, w_bf, w_bf, hg, jnp.asarray(bias), jnp.asarray(bias0))


def _proj_rest_kernel(x_ref, g1_ref, w_ref, o_ref, *, tm, rc, tn):
    silu_lo = 2 * RET_QK_W + RET_V_W
    sig_lo = silu_lo + RET_V_W
    for kc in range(tm // rc):
        rows = slice(kc * rc, (kc + 1) * rc)
        xn = _rms_rows(x_ref[rows, :], g1_ref[...]).astype(BF16)
        for ct in range(o_ref.shape[1] // tn):
            cs = slice(ct * tn, (ct + 1) * tn)
            acc = jnp.dot(xn, w_ref[:, 3 * ATT_W + ct * tn:3 * ATT_W + (ct + 1) * tn], preferred_element_type=F32)
            if ct * tn >= sig_lo:
                acc = _sigmoid(acc)
            elif ct * tn >= silu_lo:
                acc = acc * _sigmoid(acc)
            o_ref[rows, cs] = acc.astype(BF16)


def _proj_rest(x2, g1, w_bf, tm, rc, tn):
    T = x2.shape[0]
    n = w_bf.shape[1] - 3 * ATT_W
    return pl.pallas_call(
        functools.partial(_proj_rest_kernel, tm=tm, rc=rc, tn=tn),
        grid=(T // tm,),
        in_specs=[
            pl.BlockSpec((tm, D_MODEL), lambda i: (i, 0)),
            _const_spec((1, D_MODEL)),
            _const_spec(w_bf.shape),
        ],
        out_specs=pl.BlockSpec((tm, n), lambda i: (i, 0)),
        out_shape=jax.ShapeDtypeStruct((T, n), BF16),
        compiler_params=_cparams(("parallel",)),
        name="proj_rest",
    )(x2, g1, w_bf)


def _retention_tables(chunk):
    H = RET_HEADS
    log_g = np.log(1.0 - 2.0 ** (-5.0 - np.arange(H, dtype=np.float64)))
    idx = np.arange(chunk, dtype=np.float64)
    diff = idx[:, None] - idx[None, :]
    scale = RET_DK ** -0.5
    dec = np.where(diff >= 0, np.exp(log_g[:, None, None] * np.maximum(diff, 0.0)), 0.0) * scale
    xi = np.exp(log_g[:, None] * (idx[None, :] + 1.0))
    zeta = np.exp(log_g[:, None] * (chunk - 1.0 - idx[None, :])) * scale
    gch = np.exp(log_g * chunk)
    return (jnp.asarray(dec, F32),
            jnp.asarray(np.broadcast_to(xi[:, :, None], (H, chunk, RET_DV)), F32),
            jnp.asarray(np.broadcast_to(zeta[:, :, None], (H, chunk, RET_DK)), F32),
            jnp.asarray(np.broadcast_to(gch[:, None, None], (H, 1, RET_DV)), F32))


def _retention_chunk(q, k, v, sg, state_ref, h, gng, gnb, dec_ref, xi_ref, zeta_ref, gch_ref):
    s = _nt_dot(q, k) * dec_ref[h]
    inner = jnp.dot(s.astype(BF16), v, preferred_element_type=F32)
    st = state_ref[h]
    cross = jnp.dot(q, st.astype(BF16), preferred_element_type=F32) * xi_ref[h]
    kz = (k.astype(F32) * zeta_ref[h]).astype(BF16)
    state_ref[h] = st * gch_ref[h] + _tn_dot(kz, v)
    o = inner + cross
    mu = jnp.mean(o, axis=-1, keepdims=True)
    oc = o - mu
    var = jnp.mean(oc * oc, axis=-1, keepdims=True)
    return (oc * lax.rsqrt(var + EPS) * gng + gnb) * sg.astype(F32)


def _mix_kernel(x_ref, o1_ref, o2_ref, o3_ref, l1_ref, l2_ref, l3_ref, r_ref,
                wa_ref, wb_ref, wo_ref, gng_ref, gnb_ref, dec_ref, xi_ref, zeta_ref, gch_ref, out_ref,
                oa_ref, or_ref, so2_ref, so3_ref, sl2_ref, sl3_ref, state_ref, *, tm, rc):
    @pl.when(pl.program_id(1) == 0)
    def _():
        state_ref[...] = jnp.zeros_like(state_ref)

    v_lo = 2 * RET_QK_W
    sg_lo = v_lo + RET_V_W
    ga_lo = sg_lo + RET_V_W
    gb_lo = ga_lo + D_MODEL
    for kc in range(tm // rc):
        rows = slice(kc * rc, (kc + 1) * rc)
        for (o_ref, l_ref, so_ref, sl_ref, dil) in ((o2_ref, l2_ref, so2_ref, sl2_ref, ATT_GROUPS[1][1]),
                                                    (o3_ref, l3_ref, so3_ref, sl3_ref, ATT_GROUPS[2][1])):
            n = rc // dil
            src = slice(kc * n, (kc + 1) * n)
            for r in range(dil):
                blk = o_ref[0, r, src, :].astype(F32)
                for h in range(HPG):
                    so_ref[h, pl.ds(kc * rc + r, n, stride=dil), :] = blk[:, h * DH:(h + 1) * DH]
                sl_ref[pl.ds(kc * rc + r, n, stride=dil), :] = l_ref[0, r, src, :]

        st = (l1_ref[0, 0, rows, :], sl2_ref[rows, :], sl3_ref[rows, :])
        mx = jnp.maximum(jnp.maximum(st[0], st[1]), st[2])
        wt = [jnp.exp2(t - mx) for t in st]
        for h in range(HPG):
            hs = slice(h * DH, (h + 1) * DH)
            cm = h * STAT_W
            cd = cm + STAT_W // 2
            wcol = [w[:, cm:cm + 1] for w in wt]
            den = wcol[0] * st[0][:, cd:cd + 1] + wcol[1] * st[1][:, cd:cd + 1] + wcol[2] * st[2][:, cd:cd + 1]
            inv = 1.0 / den
            oa = ((wcol[0] * inv) * o1_ref[0, 0, rows, hs].astype(F32)
                  + (wcol[1] * inv) * so2_ref[h, rows, :]
                  + (wcol[2] * inv) * so3_ref[h, rows, :])
            oa_ref[rows, hs] = oa.astype(BF16)

        for h in range(RET_HEADS):
            qs = slice(h * RET_DK, (h + 1) * RET_DK)
            ks = slice(RET_QK_W + h * RET_DK, RET_QK_W + (h + 1) * RET_DK)
            vs = slice(v_lo + h * RET_DV, v_lo + (h + 1) * RET_DV)
            gs = slice(sg_lo + h * RET_DV, sg_lo + (h + 1) * RET_DV)
            ns = slice(h * RET_DV, (h + 1) * RET_DV)
            y_h = _retention_chunk(r_ref[rows, qs], r_ref[rows, ks], r_ref[rows, vs], r_ref[rows, gs], state_ref, h,
                                   gng_ref[:, ns], gnb_ref[:, ns], dec_ref, xi_ref, zeta_ref, gch_ref)
            or_ref[rows, ns] = y_h.astype(BF16)

        ya = jnp.dot(oa_ref[rows, :], wa_ref[...], preferred_element_type=F32)
        yb = jnp.dot(or_ref[rows, :], wb_ref[...], preferred_element_type=F32)
        ga = r_ref[rows, ga_lo:gb_lo].astype(F32)
        gb = r_ref[rows, gb_lo:].astype(F32)
        y = (ga * ya + gb * yb).astype(BF16)
        out_ref[rows, :] = x_ref[rows, :] + jnp.dot(y, wo_ref[...], preferred_element_type=F32)


def _mix(x2, o_att, st_att, r_mat, wa, wb, wo, gn_g, gn_b, batch, seq, tm, rc):
    T = x2.shape[0]
    nt = seq // tm
    tables = _retention_tables(rc)

    def row(w):
        return pl.BlockSpec((tm, w), lambda b, i: (b * nt + i, 0))

    def sub(w, gi):
        dil = ATT_GROUPS[gi][1]
        return pl.BlockSpec((1, dil, tm // dil, w), lambda b, i: (b, 0, i, 0))

    return pl.pallas_call(
        functools.partial(_mix_kernel, tm=tm, rc=rc),
        grid=(batch, nt),
        in_specs=[row(D_MODEL), sub(GRP_W, 0), sub(GRP_W, 1), sub(GRP_W, 2), sub(DH, 0), sub(DH, 1), sub(DH, 2),
                  row(r_mat.shape[1]),
                  _const_spec(wa.shape), _const_spec(wb.shape), _const_spec(wo.shape),
                  _const_spec(gn_g.shape), _const_spec(gn_b.shape)] + [_const_spec(t.shape) for t in tables],
        out_specs=row(D_MODEL),
        out_shape=jax.ShapeDtypeStruct((T, D_MODEL), F32),
        scratch_shapes=[pltpu.VMEM((tm, GRP_W), BF16), pltpu.VMEM((tm, RET_V_W), BF16),
                        pltpu.VMEM((HPG, tm, DH), F32), pltpu.VMEM((HPG, tm, DH), F32),
                        pltpu.VMEM((tm, DH), F32), pltpu.VMEM((tm, DH), F32),
                        pltpu.VMEM((RET_HEADS, RET_DK, RET_DV), F32)],
        compiler_params=_cparams(("parallel", "arbitrary")),
        name="mix",
    )(x2, *o_att, *st_att, r_mat, wa, wb, wo, gn_g, gn_b, *tables)


def _mlp_kernel(x_ref, g2_ref, wu_ref, wd_ref, out_ref, *, tm, rc, ff_chunk):
    for kc in range(tm // rc):
        rows = slice(kc * rc, (kc + 1) * rc)
        x1 = x_ref[rows, :]
        xn2 = _rms_rows(x1, g2_ref[...]).astype(BF16)
        acc = x1
        for c in range(D_FF // ff_chunk):
            cs = slice(c * ff_chunk, (c + 1) * ff_chunk)
            hcol = jnp.maximum(jnp.dot(xn2, wu_ref[:, cs], preferred_element_type=F32), 0.0)
            acc = acc + jnp.dot((hcol * hcol).astype(BF16), wd_ref[cs, :], preferred_element_type=F32)
        out_ref[rows, :] = acc


def _mlp(x1, g2, wu, wd, tm, rc, ff_chunk):
    T = x1.shape[0]
    return pl.pallas_call(
        functools.partial(_mlp_kernel, tm=tm, rc=rc, ff_chunk=ff_chunk),
        grid=(T // tm,),
        in_specs=[pl.BlockSpec((tm, D_MODEL), lambda i: (i, 0)), _const_spec((1, D_MODEL)),
                  _const_spec(wu.shape), _const_spec(wd.shape)],
        out_specs=pl.BlockSpec((tm, D_MODEL), lambda i: (i, 0)),
        out_shape=jax.ShapeDtypeStruct((T, D_MODEL), F32),
        compiler_params=_cparams(("parallel",)),
        name="mlp",
    )(x1, g2, wu, wd)


def kernel(x, norm1_g, w_in, q_norm_g, k_norm_g, ret_gn_g, ret_gn_b, w_proj_a, w_proj_b, w_out, norm2_g, w_up, w_down):
    B, S, D = x.shape
    T = B * S
    depth = w_in.shape[0]
    x2 = x.reshape(T, D)
    for l in range(depth):
        w_bf = w_in[l].astype(BF16)
        g1 = norm1_g[l].reshape(1, D)
        qg = q_norm_g[l] * (DH ** -0.5 * LOG2E)
        kg = k_norm_g[l]

        o_att, st_att = [], []
        for gi, (_, dil) in enumerate(ATT_GROUPS):
            hs = slice(gi * HPG, (gi + 1) * HPG)
            hg = jnp.concatenate([qg[hs].reshape(1, GRP_W), kg[hs].reshape(1, GRP_W)], axis=1)
            o, st = _att_branch(x2, g1, w_bf, hg, gi, B, S, tm=max(1024, ATT_BLOCK * dil), rc=256)
            o_att.append(o)
            st_att.append(st)

        r_mat = _proj_rest(x2, g1, w_bf, tm=512, rc=256, tn=1024)

        x1 = _mix(x2, o_att, st_att, r_mat,
                  w_proj_a[l].astype(BF16), w_proj_b[l].astype(BF16), w_out[l].astype(BF16),
                  ret_gn_g[l].reshape(1, RET_V_W), ret_gn_b[l].reshape(1, RET_V_W), B, S, tm=512, rc=256)
        x2 = _mlp(x1, norm2_g[l].reshape(1, D), w_up[l].astype(BF16), w_down[l].astype(BF16),
                  tm=1024, rc=256, ff_chunk=1024)
    return x2.reshape(B, S, D)
```

```python
import functools

import numpy as np
import jax
import jax.numpy as jnp
from jax import lax
from jax.experimental import pallas as pl
from jax.experimental.pallas import tpu as pltpu

F32 = jnp.float32
BF16 = jnp.bfloat16

D_MODEL = 1024
ATT_GROUPS = ((128, 1), (512, 4), (2048, 16))
HPG = 4
ATT_HEADS = 12
DH = 128
LANE = 128
NSLAB = D_MODEL // LANE
ATT_BLOCK = 128
ATT_W = ATT_HEADS * DH
GRP_W = HPG * DH
STAT_W = LANE // HPG
RET_HEADS = 4
RET_DK = 256
RET_DV = 512
RET_QK_W = 1024
RET_V_W = 2048
RET_CHUNK = 256
D_FF = 4096
EPS = 1e-6
NEG = -1e30
LOG2E = float(np.log2(np.e))

VMEM_LIMIT = 56 * 1024 * 1024


def _cparams(sem):
    return pltpu.CompilerParams(dimension_semantics=sem, vmem_limit_bytes=VMEM_LIMIT)


def _const_spec(shape, index=None):
    nd = len(shape)
    index = (0,) * nd if index is None else index
    return pl.BlockSpec(shape, lambda *_: index, pipeline_mode=pl.Buffered(1))


def _nt_dot(a, b):
    return lax.dot_general(a, b, (((1,), (1,)), ((), ())), preferred_element_type=F32)


def _tn_dot(a, b):
    return lax.dot_general(a, b, (((0,), (0,)), ((), ())), preferred_element_type=F32)


def _rms_rows(x, g):
    return x * lax.rsqrt(jnp.mean(x * x, axis=-1, keepdims=True) + EPS) * g


def _sigmoid(t):
    return 1.0 / (1.0 + jnp.exp(-t))


def _num_x_refs(dil):
    return 1 if dil == 1 else NSLAB


def _att_kernel(*refs, dil, tm, rc):
    nx = _num_x_refs(dil)
    x_refs = refs[:nx]
    (g1_ref, wq_ref, wk_ref, wv_ref, hg_ref, bias_ref, bias0_ref,
     o_ref, st_ref, qkv_ref, carry_ref) = refs[nx:nx + 11]
    n = tm // dil
    first = pl.program_id(1) == 0

    @pl.when(first)
    def _():
        carry_ref[...] = jnp.zeros_like(carry_ref)

    if dil == 16:
        xa_ref, = refs[nx + 11:]
        n4 = tm // 4
        for c in range(NSLAB):
            for lo in range(4):
                xa_ref[c, lo * n4:(lo + 1) * n4, :] = x_refs[c][pl.ds(lo, n4, stride=4), :]

    def load_rows(c, r, l0, ln):
        if dil == 1:
            return x_refs[0][l0:l0 + ln, c * LANE:(c + 1) * LANE]
        if dil == 16:
            return xa_ref[c, pl.ds((r % 4) * n4 + r // 4 + 4 * l0, ln, stride=4), :]
        return x_refs[c][pl.ds(r + l0 * dil, ln, stride=dil), :]

    def project(kc):
        pieces, p = [], kc * rc
        while p < (kc + 1) * rc:
            r, l0 = divmod(p, n)
            ln = min(n - l0, (kc + 1) * rc - p)
            xs = [load_rows(c, r, l0, ln) for c in range(NSLAB)]
            ss = xs[0] * xs[0]
            for c in range(1, NSLAB):
                ss = ss + xs[c] * xs[c]
            scale = lax.rsqrt(jnp.sum(ss, axis=-1, keepdims=True) * (1.0 / D_MODEL) + EPS)
            pieces.append(jnp.concatenate(
                [(xs[c] * scale * g1_ref[:, c * LANE:(c + 1) * LANE]).astype(BF16) for c in range(NSLAB)], axis=1))
            p += ln
        xn = pieces[0] if len(pieces) == 1 else jnp.concatenate(pieces, axis=0)
        rows = slice(kc * rc, (kc + 1) * rc)
        for part, w_ref in enumerate((wq_ref, wk_ref)):
            acc = jnp.dot(xn, w_ref[...], preferred_element_type=F32)
            for h in range(HPG):
                hs = slice(h * DH, (h + 1) * DH)
                t = acc[:, hs]
                rr = lax.rsqrt(jnp.sum(t * t, axis=-1, keepdims=True) * (1.0 / DH) + EPS)
                gs = slice(part * GRP_W + h * DH, part * GRP_W + (h + 1) * DH)
                qkv_ref[rows, gs] = (t * rr * hg_ref[:, gs]).astype(BF16)
        qkv_ref[rows, 2 * GRP_W:] = jnp.dot(xn, wv_ref[...], preferred_element_type=F32).astype(BF16)

    lane = lax.broadcasted_iota(jnp.int32, (ATT_BLOCK, DH), 1)
    ones = jnp.ones((2 * ATT_BLOCK, DH), BF16)

    def attend(p0):
        r, l0 = divmod(p0, n)
        qrows = slice(p0, p0 + ATT_BLOCK)
        st_blk = None
        for h in range(HPG):
            hs = slice(h * DH, (h + 1) * DH)
            ks = slice(GRP_W + h * DH, GRP_W + (h + 1) * DH)
            vs = slice(2 * GRP_W + h * DH, 2 * GRP_W + (h + 1) * DH)
            q = qkv_ref[qrows, hs]
            if l0 == 0:
                k = jnp.concatenate([carry_ref[r, :, hs], qkv_ref[qrows, ks]], axis=0)
                v = jnp.concatenate([carry_ref[r, :, ks], qkv_ref[qrows, vs]], axis=0)
                b = jnp.where(first, bias0_ref[h], bias_ref[h])
            else:
                krows = slice(p0 - ATT_BLOCK, p0 + ATT_BLOCK)
                k = qkv_ref[krows, ks]
                v = qkv_ref[krows, vs]
                b = bias_ref[h]
            s = _nt_dot(q, k) + b
            m = jnp.max(s, axis=-1, keepdims=True)
            p = jnp.exp2(s - m).astype(BF16)
            oe = jnp.dot(p, jnp.concatenate([v, ones], axis=1), preferred_element_type=F32)
            o_ref[0, r, l0:l0 + ATT_BLOCK, hs] = oe[:, :DH].astype(BF16)
            lo = h * STAT_W
            mb = jnp.broadcast_to(m, (ATT_BLOCK, DH))
            st_blk = mb if h == 0 else jnp.where(lane >= lo, mb, st_blk)
            st_blk = jnp.where(lane >= lo + STAT_W // 2, oe[:, DH:], st_blk)
        st_ref[0, r, l0:l0 + ATT_BLOCK, :] = st_blk

    for kc in range(tm // rc):
        project(kc)
    for t in range(tm // ATT_BLOCK):
        attend(t * ATT_BLOCK)

    for r in range(dil):
        carry_ref[r] = qkv_ref[(r + 1) * n - ATT_BLOCK:(r + 1) * n, GRP_W:]


def _att_branch(x2, g1, w_bf, hg, gi, batch, seq, tm, rc):
    window, dil = ATT_GROUPS[gi]
    nt = seq // tm
    n = tm // dil

    slopes = 2.0 ** (-8.0 * np.arange(1, ATT_HEADS + 1, dtype=np.float32) / ATT_HEADS)
    slopes = slopes[gi * HPG:(gi + 1) * HPG]
    qi = np.arange(ATT_BLOCK)[:, None]
    kj = np.arange(2 * ATT_BLOCK)[None, :]
    dist = ATT_BLOCK + qi - kj
    valid = (dist >= 0) & (dist <= window // dil)
    alibi = -slopes[:, None, None] * (dist * dil).astype(np.float32)[None] * np.float32(LOG2E)
    bias = np.where(valid[None], alibi, NEG).astype(np.float32)
    bias0 = np.where((kj >= ATT_BLOCK)[None], bias, NEG).astype(np.float32)

    nx = _num_x_refs(dil)
    slab = [pl.BlockSpec((tm, D_MODEL // nx), functools.partial(lambda b, i, c: (b * nt + i, c), c=c))
            for c in range(nx)]
    ncb = ATT_W // GRP_W
    w_specs = [_const_spec((D_MODEL, GRP_W), (0, part * ncb + gi)) for part in range(3)]
    gather = [pltpu.VMEM((NSLAB, tm, LANE), F32)] if dil == 16 else []
    return pl.pallas_call(
        functools.partial(_att_kernel, dil=dil, tm=tm, rc=rc),
        grid=(batch, nt),
        in_specs=slab + [_const_spec((1, D_MODEL))] + w_specs
        + [_const_spec(hg.shape), _const_spec(bias.shape), _const_spec(bias0.shape)],
        out_specs=[pl.BlockSpec((1, dil, n, GRP_W), lambda b, i: (b, 0, i, 0)),
                   pl.BlockSpec((1, dil, n, DH), lambda b, i: (b, 0, i, 0))],
        out_shape=[jax.ShapeDtypeStruct((batch, dil, seq // dil, GRP_W), BF16),
                   jax.ShapeDtypeStruct((batch, dil, seq // dil, DH), F32)],
        scratch_shapes=[pltpu.VMEM((tm, 3 * GRP_W), BF16), pltpu.VMEM((dil, ATT_BLOCK, 2 * GRP_W), BF16)] + gather,
        compiler_params=_cparams(("parallel", "arbitrary")),
        name=f"att_d{dil}",
    )(*([x2] * nx), g1, w_bf, w_bf, w_bf, hg, jnp.asarray(bias), jnp.asarray(bias0))


def _proj_rest_kernel(x_ref, g1_ref, w_ref, tq_ref, tk_ref, o_ref, *, tm, rc, tn):
    assert tn == RET_QK_W
    silu_lo = 2 * RET_QK_W + RET_V_W
    sig_lo = silu_lo + RET_V_W
    for kc in range(tm // rc):
        rows = slice(kc * rc, (kc + 1) * rc)
        xn = _rms_rows(x_ref[rows, :], g1_ref[...]).astype(BF16)
        for ct in range(o_ref.shape[1] // tn):
            cs = slice(ct * tn, (ct + 1) * tn)
            acc = jnp.dot(xn, w_ref[:, 3 * ATT_W + ct * tn:3 * ATT_W + (ct + 1) * tn], preferred_element_type=F32)
            if ct == 0:
                acc = acc * tq_ref[...]
            elif ct == 1:
                acc = acc * tk_ref[...]
            elif ct * tn >= sig_lo:
                acc = _sigmoid(acc)
            elif ct * tn >= silu_lo:
                acc = acc * _sigmoid(acc)
            o_ref[rows, cs] = acc.astype(BF16)


def _proj_rest(x2, g1, w_bf, tm, rc, tn):
    T = x2.shape[0]
    n = w_bf.shape[1] - 3 * ATT_W
    tabq, tabk = _retention_row_tables(rc)
    return pl.pallas_call(
        functools.partial(_proj_rest_kernel, tm=tm, rc=rc, tn=tn),
        grid=(T // tm,),
        in_specs=[
            pl.BlockSpec((tm, D_MODEL), lambda i: (i, 0)),
            _const_spec((1, D_MODEL)),
            _const_spec(w_bf.shape),
            _const_spec(tabq.shape),
            _const_spec(tabk.shape),
        ],
        out_specs=pl.BlockSpec((tm, n), lambda i: (i, 0)),
        out_shape=jax.ShapeDtypeStruct((T, n), BF16),
        compiler_params=_cparams(("parallel",)),
        name="proj_rest",
    )(x2, g1, w_bf, tabq, tabk)


def _retention_log_gamma():
    return np.log(1.0 - 2.0 ** (-5.0 - np.arange(RET_HEADS, dtype=np.float64)))


def _retention_row_tables(chunk):
    pos = np.arange(chunk, dtype=np.float64)[:, None, None] + 1.0
    lg = _retention_log_gamma()[None, :, None]
    shape = (chunk, RET_HEADS, RET_DK)
    tq = np.broadcast_to(np.exp(lg * pos), shape).reshape(chunk, RET_QK_W)
    tk = np.broadcast_to(np.exp(-lg * pos) * RET_DK ** -0.5, shape).reshape(chunk, RET_QK_W)
    return jnp.asarray(tq, F32), jnp.asarray(tk, F32)


def _retention_chunk_tables(chunk):
    idx = np.arange(chunk)
    mask = (idx[:, None] >= idx[None, :]).astype(np.float32)
    gch = np.exp(_retention_log_gamma() * chunk)
    return jnp.asarray(mask), jnp.asarray(np.broadcast_to(gch[:, None, None], (RET_HEADS, 1, RET_DV)), F32)


def _retention_chunk(qx, kx, v, sg, state_ref, h, gng, gnb, mask_ref, gch_ref):
    s = _nt_dot(qx, kx) * mask_ref[...]
    st = state_ref[h]
    o = (jnp.dot(s.astype(BF16), v, preferred_element_type=F32)
         + jnp.dot(qx, st.astype(BF16), preferred_element_type=F32))
    state_ref[h] = (st + _tn_dot(kx, v)) * gch_ref[h]
    mu = jnp.mean(o, axis=-1, keepdims=True)
    oc = o - mu
    var = jnp.mean(oc * oc, axis=-1, keepdims=True)
    return (oc * lax.rsqrt(var + EPS) * gng + gnb) * sg.astype(F32)


def _mix_kernel(x_ref, o1_ref, o2_ref, o3_ref, l1_ref, l2_ref, l3_ref, r_ref,
                wa_ref, wb_ref, wo_ref, gng_ref, gnb_ref, mask_ref, gch_ref, out_ref,
                oa_ref, or_ref, so2_ref, so3_ref, sl2_ref, sl3_ref, state_ref, *, tm, rc):
    @pl.when(pl.program_id(1) == 0)
    def _():
        state_ref[...] = jnp.zeros_like(state_ref)

    v_lo = 2 * RET_QK_W
    sg_lo = v_lo + RET_V_W
    ga_lo = sg_lo + RET_V_W
    gb_lo = ga_lo + D_MODEL
    for kc in range(tm // rc):
        rows = slice(kc * rc, (kc + 1) * rc)
        for (o_ref, l_ref, so_ref, sl_ref, dil) in ((o2_ref, l2_ref, so2_ref, sl2_ref, ATT_GROUPS[1][1]),
                                                    (o3_ref, l3_ref, so3_ref, sl3_ref, ATT_GROUPS[2][1])):
            n = rc // dil
            src = slice(kc * n, (kc + 1) * n)
            for r in range(dil):
                blk = o_ref[0, r, src, :].astype(F32)
                for h in range(HPG):
                    so_ref[h, pl.ds(kc * rc + r, n, stride=dil), :] = blk[:, h * DH:(h + 1) * DH]
                sl_ref[pl.ds(kc * rc + r, n, stride=dil), :] = l_ref[0, r, src, :]

        st = (l1_ref[0, 0, rows, :], sl2_ref[rows, :], sl3_ref[rows, :])
        mx = jnp.maximum(jnp.maximum(st[0], st[1]), st[2])
        wt = [jnp.exp2(t - mx) for t in st]
        for h in range(HPG):
            hs = slice(h * DH, (h + 1) * DH)
            cm = h * STAT_W
            cd = cm + STAT_W // 2
            wcol = [w[:, cm:cm + 1] for w in wt]
            den = wcol[0] * st[0][:, cd:cd + 1] + wcol[1] * st[1][:, cd:cd + 1] + wcol[2] * st[2][:, cd:cd + 1]
            inv = 1.0 / den
            oa = ((wcol[0] * inv) * o1_ref[0, 0, rows, hs].astype(F32)
                  + (wcol[1] * inv) * so2_ref[h, rows, :]
                  + (wcol[2] * inv) * so3_ref[h, rows, :])
            oa_ref[rows, hs] = oa.astype(BF16)

        for h in range(RET_HEADS):
            qs = slice(h * RET_DK, (h + 1) * RET_DK)
            ks = slice(RET_QK_W + h * RET_DK, RET_QK_W + (h + 1) * RET_DK)
            vs = slice(v_lo + h * RET_DV, v_lo + (h + 1) * RET_DV)
            gs = slice(sg_lo + h * RET_DV, sg_lo + (h + 1) * RET_DV)
            ns = slice(h * RET_DV, (h + 1) * RET_DV)
            y_h = _retention_chunk(r_ref[rows, qs], r_ref[rows, ks], r_ref[rows, vs], r_ref[rows, gs], state_ref, h,
                                   gng_ref[:, ns], gnb_ref[:, ns], mask_ref, gch_ref)
            or_ref[rows, ns] = y_h.astype(BF16)

        ya = jnp.dot(oa_ref[rows, :], wa_ref[...], preferred_element_type=F32)
        yb = jnp.dot(or_ref[rows, :], wb_ref[...], preferred_element_type=F32)
        ga = r_ref[rows, ga_lo:gb_lo].astype(F32)
        gb = r_ref[rows, gb_lo:].astype(F32)
        y = (ga * ya + gb * yb).astype(BF16)
        out_ref[rows, :] = x_ref[rows, :] + jnp.dot(y, wo_ref[...], preferred_element_type=F32)


def _mix(x2, o_att, st_att, r_mat, wa, wb, wo, gn_g, gn_b, batch, seq, tm, rc):
    T = x2.shape[0]
    nt = seq // tm
    tables = _retention_chunk_tables(rc)

    def row(w):
        return pl.BlockSpec((tm, w), lambda b, i: (b * nt + i, 0))

    def sub(w, gi):
        dil = ATT_GROUPS[gi][1]
        return pl.BlockSpec((1, dil, tm // dil, w), lambda b, i: (b, 0, i, 0))

    return pl.pallas_call(
        functools.partial(_mix_kernel, tm=tm, rc=rc),
        grid=(batch, nt),
        in_specs=[row(D_MODEL), sub(GRP_W, 0), sub(GRP_W, 1), sub(GRP_W, 2), sub(DH, 0), sub(DH, 1), sub(DH, 2),
                  row(r_mat.shape[1]),
                  _const_spec(wa.shape), _const_spec(wb.shape), _const_spec(wo.shape),
                  _const_spec(gn_g.shape), _const_spec(gn_b.shape)] + [_const_spec(t.shape) for t in tables],
        out_specs=row(D_MODEL),
        out_shape=jax.ShapeDtypeStruct((T, D_MODEL), F32),
        scratch_shapes=[pltpu.VMEM((tm, GRP_W), BF16), pltpu.VMEM((tm, RET_V_W), BF16),
                        pltpu.VMEM((HPG, tm, DH), F32), pltpu.VMEM((HPG, tm, DH), F32),
                        pltpu.VMEM((tm, DH), F32), pltpu.VMEM((tm, DH), F32),
                        pltpu.VMEM((RET_HEADS, RET_DK, RET_DV), F32)],
        compiler_params=_cparams(("parallel", "arbitrary")),
        name="mix",
    )(x2, *o_att, *st_att, r_mat, wa, wb, wo, gn_g, gn_b, *tables)


def _mlp_kernel(x_ref, g2_ref, wu_ref, wd_ref, out_ref, *, tm, rc, ff_chunk):
    for kc in range(tm // rc):
        rows = slice(kc * rc, (kc + 1) * rc)
        x1 = x_ref[rows, :]
        xn2 = _rms_rows(x1, g2_ref[...]).astype(BF16)
        acc = x1
        for c in range(D_FF // ff_chunk):
            cs = slice(c * ff_chunk, (c + 1) * ff_chunk)
            hcol = jnp.maximum(jnp.dot(xn2, wu_ref[:, cs], preferred_element_type=F32), 0.0)
            acc = acc + jnp.dot((hcol * hcol).astype(BF16), wd_ref[cs, :], preferred_element_type=F32)
        out_ref[rows, :] = acc


def _mlp(x1, g2, wu, wd, tm, rc, ff_chunk):
    T = x1.shape[0]
    return pl.pallas_call(
        functools.partial(_mlp_kernel, tm=tm, rc=rc, ff_chunk=ff_chunk),
        grid=(T // tm,),
        in_specs=[pl.BlockSpec((tm, D_MODEL), lambda i: (i, 0)), _const_spec((1, D_MODEL)),
                  _const_spec(wu.shape), _const_spec(wd.shape)],
        out_specs=pl.BlockSpec((tm, D_MODEL), lambda i: (i, 0)),
        out_shape=jax.ShapeDtypeStruct((T, D_MODEL), F32),
        compiler_params=_cparams(("parallel",)),
        name="mlp",
    )(x1, g2, wu, wd)


def kernel(x, norm1_g, w_in, q_norm_g, k_norm_g, ret_gn_g, ret_gn_b, w_proj_a, w_proj_b, w_out, norm2_g, w_up, w_down):
    B, S, D = x.shape
    T = B * S
    depth = w_in.shape[0]
    x2 = x.reshape(T, D)
    for l in range(depth):
        w_bf = w_in[l].astype(BF16)
        g1 = norm1_g[l].reshape(1, D)
        qg = q_norm_g[l] * (DH ** -0.5 * LOG2E)
        kg = k_norm_g[l]

        o_att, st_att = [], []
        for gi, (_, dil) in enumerate(ATT_GROUPS):
            hs = slice(gi * HPG, (gi + 1) * HPG)
            hg = jnp.concatenate([qg[hs].reshape(1, GRP_W), kg[hs].reshape(1, GRP_W)], axis=1)
            o, st = _att_branch(x2, g1, w_bf, hg, gi, B, S, tm=max(1024, ATT_BLOCK * dil), rc=256)
            o_att.append(o)
            st_att.append(st)

        r_mat = _proj_rest(x2, g1, w_bf, tm=512, rc=RET_CHUNK, tn=RET_QK_W)

        x1 = _mix(x2, o_att, st_att, r_mat,
                  w_proj_a[l].astype(BF16), w_proj_b[l].astype(BF16), w_out[l].astype(BF16),
                  ret_gn_g[l].reshape(1, RET_V_W), ret_gn_b[l].reshape(1, RET_V_W), B, S, tm=512, rc=RET_CHUNK)
        x2 = _mlp(x1, norm2_g[l].reshape(1, D), w_up[l].astype(BF16), w_down[l].astype(BF16),
                  tm=1024, rc=256, ff_chunk=1024)
    return x2.reshape(B, S, D)
```

```python
import functools

import numpy as np
import jax
import jax.numpy as jnp
from jax import lax
from jax.experimental import pallas as pl
from jax.experimental.pallas import tpu as pltpu

F32 = jnp.float32
BF16 = jnp.bfloat16

D_MODEL = 1024
ATT_GROUPS = ((128, 1), (512, 4), (2048, 16))
HPG = 4
ATT_HEADS = 12
DH = 128
LANE = 128
NSLAB = D_MODEL // LANE
ATT_BLOCK = 128
ATT_W = ATT_HEADS * DH
GRP_W = HPG * DH
STAT_W = LANE // HPG
RET_HEADS = 4
RET_DK = 256
RET_DV = 512
RET_QK_W = 1024
RET_V_W = 2048
RET_CHUNK = 256
D_FF = 4096
EPS = 1e-6
NEG = -1e30
LOG2E = float(np.log2(np.e))

VMEM_LIMIT = 56 * 1024 * 1024


def _cparams(sem):
    return pltpu.CompilerParams(dimension_semantics=sem, vmem_limit_bytes=VMEM_LIMIT)


def _const_spec(shape, index=None):
    nd = len(shape)
    index = (0,) * nd if index is None else index
    return pl.BlockSpec(shape, lambda *_: index, pipeline_mode=pl.Buffered(1))


def _nt_dot(a, b):
    return lax.dot_general(a, b, (((1,), (1,)), ((), ())), preferred_element_type=F32)


def _tn_dot(a, b):
    return lax.dot_general(a, b, (((0,), (0,)), ((), ())), preferred_element_type=F32)


def _rms_rows(x, g):
    return x * lax.rsqrt(jnp.mean(x * x, axis=-1, keepdims=True) + EPS) * g


def _sigmoid(t):
    return 1.0 / (1.0 + jnp.exp(-t))


def _num_x_refs(dil):
    return 1 if dil == 1 else NSLAB


def _att_kernel(*refs, dil, tm, rc):
    nx = _num_x_refs(dil)
    x_refs = refs[:nx]
    (g1_ref, wq_ref, wk_ref, wv_ref, hg_ref, bias_ref, bias0_ref,
     o_ref, st_ref, qkv_ref, carry_ref) = refs[nx:nx + 11]
    n = tm // dil
    first = pl.program_id(1) == 0

    @pl.when(first)
    def _():
        carry_ref[...] = jnp.zeros_like(carry_ref)

    if dil == 16:
        xa_ref, = refs[nx + 11:]
        n4 = tm // 4
        for c in range(NSLAB):
            for lo in range(4):
                xa_ref[c, lo * n4:(lo + 1) * n4, :] = x_refs[c][pl.ds(lo, n4, stride=4), :]

    def load_rows(c, r, l0, ln):
        if dil == 1:
            return x_refs[0][l0:l0 + ln, c * LANE:(c + 1) * LANE]
        if dil == 16:
            return xa_ref[c, pl.ds((r % 4) * n4 + r // 4 + 4 * l0, ln, stride=4), :]
        return x_refs[c][pl.ds(r + l0 * dil, ln, stride=dil), :]

    def project(kc):
        pieces, p = [], kc * rc
        while p < (kc + 1) * rc:
            r, l0 = divmod(p, n)
            ln = min(n - l0, (kc + 1) * rc - p)
            xs = [load_rows(c, r, l0, ln) for c in range(NSLAB)]
            ss = xs[0] * xs[0]
            for c in range(1, NSLAB):
                ss = ss + xs[c] * xs[c]
            scale = lax.rsqrt(jnp.sum(ss, axis=-1, keepdims=True) * (1.0 / D_MODEL) + EPS)
            pieces.append(jnp.concatenate(
                [(xs[c] * scale * g1_ref[:, c * LANE:(c + 1) * LANE]).astype(BF16) for c in range(NSLAB)], axis=1))
            p += ln
        xn = pieces[0] if len(pieces) == 1 else jnp.concatenate(pieces, axis=0)
        rows = slice(kc * rc, (kc + 1) * rc)
        for part, w_ref in enumerate((wq_ref, wk_ref)):
            acc = jnp.dot(xn, w_ref[...], preferred_element_type=F32)
            for h in range(HPG):
                hs = slice(h * DH, (h + 1) * DH)
                t = acc[:, hs]
                rr = lax.rsqrt(jnp.sum(t * t, axis=-1, keepdims=True) * (1.0 / DH) + EPS)
                gs = slice(part * GRP_W + h * DH, part * GRP_W + (h + 1) * DH)
                qkv_ref[rows, gs] = (t * rr * hg_ref[:, gs]).astype(BF16)
        qkv_ref[rows, 2 * GRP_W:] = jnp.dot(xn, wv_ref[...], preferred_element_type=F32).astype(BF16)

    lane = lax.broadcasted_iota(jnp.int32, (ATT_BLOCK, DH), 1)
    ones = jnp.ones((2 * ATT_BLOCK, DH), BF16)

    def attend(p0):
        r, l0 = divmod(p0, n)
        qrows = slice(p0, p0 + ATT_BLOCK)
        st_blk = None
        for h in range(HPG):
            hs = slice(h * DH, (h + 1) * DH)
            ks = slice(GRP_W + h * DH, GRP_W + (h + 1) * DH)
            vs = slice(2 * GRP_W + h * DH, 2 * GRP_W + (h + 1) * DH)
            q = qkv_ref[qrows, hs]
            if l0 == 0:
                k = jnp.concatenate([carry_ref[r, :, hs], qkv_ref[qrows, ks]], axis=0)
                v = jnp.concatenate([carry_ref[r, :, ks], qkv_ref[qrows, vs]], axis=0)
                b = jnp.where(first, bias0_ref[h], bias_ref[h])
            else:
                krows = slice(p0 - ATT_BLOCK, p0 + ATT_BLOCK)
                k = qkv_ref[krows, ks]
                v = qkv_ref[krows, vs]
                b = bias_ref[h]
            s = _nt_dot(q, k) + b
            m = jnp.max(s, axis=-1, keepdims=True)
            p = jnp.exp2(s - m).astype(BF16)
            oe = jnp.dot(p, jnp.concatenate([v, ones], axis=1), preferred_element_type=F32)
            o_ref[0, r, l0:l0 + ATT_BLOCK, hs] = oe[:, :DH].astype(BF16)
            lo = h * STAT_W
            mb = jnp.broadcast_to(m, (ATT_BLOCK, DH))
            st_blk = mb if h == 0 else jnp.where(lane >= lo, mb, st_blk)
            st_blk = jnp.where(lane >= lo + STAT_W // 2, oe[:, DH:], st_blk)
        st_ref[0, r, l0:l0 + ATT_BLOCK, :] = st_blk

    for kc in range(tm // rc):
        project(kc)
    for t in range(tm // ATT_BLOCK):
        attend(t * ATT_BLOCK)

    for r in range(dil):
        carry_ref[r] = qkv_ref[(r + 1) * n - ATT_BLOCK:(r + 1) * n, GRP_W:]


def _att_branch(x2, g1, w_bf, hg, gi, batch, seq, tm, rc):
    window, dil = ATT_GROUPS[gi]
    nt = seq // tm
    n = tm // dil

    slopes = 2.0 ** (-8.0 * np.arange(1, ATT_HEADS + 1, dtype=np.float32) / ATT_HEADS)
    slopes = slopes[gi * HPG:(gi + 1) * HPG]
    qi = np.arange(ATT_BLOCK)[:, None]
    kj = np.arange(2 * ATT_BLOCK)[None, :]
    dist = ATT_BLOCK + qi - kj
    valid = (dist >= 0) & (dist <= window // dil)
    alibi = -slopes[:, None, None] * (dist * dil).astype(np.float32)[None] * np.float32(LOG2E)
    bias = np.where(valid[None], alibi, NEG).astype(np.float32)
    bias0 = np.where((kj >= ATT_BLOCK)[None], bias, NEG).astype(np.float32)

    nx = _num_x_refs(dil)
    slab = [pl.BlockSpec((tm, D_MODEL // nx), functools.partial(lambda b, i, c: (b * nt + i, c), c=c))
            for c in range(nx)]
    ncb = ATT_W // GRP_W
    w_specs = [_const_spec((D_MODEL, GRP_W), (0, part * ncb + gi)) for part in range(3)]
    gather = [pltpu.VMEM((NSLAB, tm, LANE), F32)] if dil == 16 else []
    return pl.pallas_call(
        functools.partial(_att_kernel, dil=dil, tm=tm, rc=rc),
        grid=(batch, nt),
        in_specs=slab + [_const_spec((1, D_MODEL))] + w_specs
        + [_const_spec(hg.shape), _const_spec(bias.shape), _const_spec(bias0.shape)],
        out_specs=[pl.BlockSpec((1, dil, n, GRP_W), lambda b, i: (b, 0, i, 0)),
                   pl.BlockSpec((1, dil, n, DH), lambda b, i: (b, 0, i, 0))],
        out_shape=[jax.ShapeDtypeStruct((batch, dil, seq // dil, GRP_W), BF16),
                   jax.ShapeDtypeStruct((batch, dil, seq // dil, DH), F32)],
        scratch_shapes=[pltpu.VMEM((tm, 3 * GRP_W), BF16), pltpu.VMEM((dil, ATT_BLOCK, 2 * GRP_W), BF16)] + gather,
        compiler_params=_cparams(("parallel", "arbitrary")),
        name=f"att_d{dil}",
    )(*([x2] * nx), g1, w_bf, w_bf, w_bf, hg, jnp.asarray(bias), jnp.asarray(bias0))


def _proj_rest_kernel(x_ref, g1_ref, w_ref, tq_ref, tk_ref, o_ref, *, tm, rc, tn):
    assert tn == RET_QK_W
    silu_lo = 2 * RET_QK_W + RET_V_W
    sig_lo = silu_lo + RET_V_W
    for kc in range(tm // rc):
        rows = slice(kc * rc, (kc + 1) * rc)
        xn = _rms_rows(x_ref[rows, :], g1_ref[...]).astype(BF16)
        for ct in range(o_ref.shape[1] // tn):
            cs = slice(ct * tn, (ct + 1) * tn)
            acc = jnp.dot(xn, w_ref[:, 3 * ATT_W + ct * tn:3 * ATT_W + (ct + 1) * tn], preferred_element_type=F32)
            if ct == 0:
                acc = acc * tq_ref[...]
            elif ct == 1:
                acc = acc * tk_ref[...]
            elif ct * tn >= sig_lo:
                acc = _sigmoid(acc)
            elif ct * tn >= silu_lo:
                acc = acc * _sigmoid(acc)
            o_ref[rows, cs] = acc.astype(BF16)


def _proj_rest(x2, g1, w_bf, tm, rc, tn):
    T = x2.shape[0]
    n = w_bf.shape[1] - 3 * ATT_W
    tabq, tabk = _retention_row_tables(rc)
    return pl.pallas_call(
        functools.partial(_proj_rest_kernel, tm=tm, rc=rc, tn=tn),
        grid=(T // tm,),
        in_specs=[
            pl.BlockSpec((tm, D_MODEL), lambda i: (i, 0)),
            _const_spec((1, D_MODEL)),
            _const_spec(w_bf.shape),
            _const_spec(tabq.shape),
            _const_spec(tabk.shape),
        ],
        out_specs=pl.BlockSpec((tm, n), lambda i: (i, 0)),
        out_shape=jax.ShapeDtypeStruct((T, n), BF16),
        compiler_params=_cparams(("parallel",)),
        name="proj_rest",
    )(x2, g1, w_bf, tabq, tabk)


def _retention_log_gamma():
    return np.log(1.0 - 2.0 ** (-5.0 - np.arange(RET_HEADS, dtype=np.float64)))


def _retention_row_tables(chunk):
    pos = np.arange(chunk, dtype=np.float64)[:, None, None] + 1.0
    lg = _retention_log_gamma()[None, :, None]
    shape = (chunk, RET_HEADS, RET_DK)
    tq = np.broadcast_to(np.exp(lg * pos), shape).reshape(chunk, RET_QK_W)
    tk = np.broadcast_to(np.exp(-lg * pos) * RET_DK ** -0.5, shape).reshape(chunk, RET_QK_W)
    return jnp.asarray(tq, F32), jnp.asarray(tk, F32)


def _retention_chunk_tables(chunk):
    idx = np.arange(chunk)
    mask = (idx[:, None] >= idx[None, :]).astype(np.float32)
    gch = np.exp(_retention_log_gamma() * chunk)
    return jnp.asarray(mask), jnp.asarray(np.broadcast_to(gch[:, None, None], (RET_HEADS, 1, RET_DV)), F32)


def _retention_chunk(qx, kx, v, sg, state_ref, h, gng, gnb, mask_ref, gch_ref):
    s = _nt_dot(qx, kx) * mask_ref[...]
    st = state_ref[h]
    o = (jnp.dot(s.astype(BF16), v, preferred_element_type=F32)
         + jnp.dot(qx, st.astype(BF16), preferred_element_type=F32))
    state_ref[h] = (st + _tn_dot(kx, v)) * gch_ref[h]
    mu = jnp.mean(o, axis=-1, keepdims=True)
    oc = o - mu
    var = jnp.mean(oc * oc, axis=-1, keepdims=True)
    return (oc * lax.rsqrt(var + EPS) * gng + gnb) * sg.astype(F32)


def _mix_kernel(x_ref, o1_ref, o2_ref, o3_ref, l1_ref, l2_ref, l3_ref, r_ref,
                wa_ref, wb_ref, wo_ref, gng_ref, gnb_ref, mask_ref, gch_ref, out_ref,
                oa_ref, or_ref, so2_ref, so3_ref, sl2_ref, sl3_ref, state_ref, *, tm, rc):
    @pl.when(pl.program_id(1) == 0)
    def _():
        state_ref[...] = jnp.zeros_like(state_ref)

    v_lo = 2 * RET_QK_W
    sg_lo = v_lo + RET_V_W
    ga_lo = sg_lo + RET_V_W
    gb_lo = ga_lo + D_MODEL
    for kc in range(tm // rc):
        rows = slice(kc * rc, (kc + 1) * rc)
        for (o_ref, l_ref, so_ref, sl_ref, dil) in ((o2_ref, l2_ref, so2_ref, sl2_ref, ATT_GROUPS[1][1]),
                                                    (o3_ref, l3_ref, so3_ref, sl3_ref, ATT_GROUPS[2][1])):
            n = rc // dil
            src = slice(kc * n, (kc + 1) * n)
            for r in range(dil):
                blk = o_ref[0, r, src, :].astype(F32)
                for h in range(HPG):
                    so_ref[h, pl.ds(kc * rc + r, n, stride=dil), :] = blk[:, h * DH:(h + 1) * DH]
                sl_ref[pl.ds(kc * rc + r, n, stride=dil), :] = l_ref[0, r, src, :]

        st = (l1_ref[0, 0, rows, :], sl2_ref[rows, :], sl3_ref[rows, :])
        mx = jnp.maximum(jnp.maximum(st[0], st[1]), st[2])
        wt = [jnp.exp2(t - mx) for t in st]
        for h in range(HPG):
            hs = slice(h * DH, (h + 1) * DH)
            cm = h * STAT_W
            cd = cm + STAT_W // 2
            wcol = [w[:, cm:cm + 1] for w in wt]
            den = wcol[0] * st[0][:, cd:cd + 1] + wcol[1] * st[1][:, cd:cd + 1] + wcol[2] * st[2][:, cd:cd + 1]
            inv = 1.0 / den
            oa = ((wcol[0] * inv) * o1_ref[0, 0, rows, hs].astype(F32)
                  + (wcol[1] * inv) * so2_ref[h, rows, :]
                  + (wcol[2] * inv) * so3_ref[h, rows, :])
            oa_ref[rows, hs] = oa.astype(BF16)

        for h in range(RET_HEADS):
            qs = slice(h * RET_DK, (h + 1) * RET_DK)
            ks = slice(RET_QK_W + h * RET_DK, RET_QK_W + (h + 1) * RET_DK)
            vs = slice(v_lo + h * RET_DV, v_lo + (h + 1) * RET_DV)
            gs = slice(sg_lo + h * RET_DV, sg_lo + (h + 1) * RET_DV)
            ns = slice(h * RET_DV, (h + 1) * RET_DV)
            y_h = _retention_chunk(r_ref[rows, qs], r_ref[rows, ks], r_ref[rows, vs], r_ref[rows, gs], state_ref, h,
                                   gng_ref[:, ns], gnb_ref[:, ns], mask_ref, gch_ref)
            or_ref[rows, ns] = y_h.astype(BF16)

        ya = jnp.dot(oa_ref[rows, :], wa_ref[...], preferred_element_type=F32)
        yb = jnp.dot(or_ref[rows, :], wb_ref[...], preferred_element_type=F32)
        ga = r_ref[rows, ga_lo:gb_lo].astype(F32)
        gb = r_ref[rows, gb_lo:].astype(F32)
        y = (ga * ya + gb * yb).astype(BF16)
        out_ref[rows, :] = x_ref[rows, :] + jnp.dot(y, wo_ref[...], preferred_element_type=F32)


def _mix(x2, o_att, st_att, r_mat, wa, wb, wo, gn_g, gn_b, batch, seq, tm, rc):
    T = x2.shape[0]
    nt = seq // tm
    tables = _retention_chunk_tables(rc)

    def row(w):
        return pl.BlockSpec((tm, w), lambda b, i: (b * nt + i, 0))

    def sub(w, gi):
        dil = ATT_GROUPS[gi][1]
        return pl.BlockSpec((1, dil, tm // dil, w), lambda b, i: (b, 0, i, 0))

    return pl.pallas_call(
        functools.partial(_mix_kernel, tm=tm, rc=rc),
        grid=(batch, nt),
        in_specs=[row(D_MODEL), sub(GRP_W, 0), sub(GRP_W, 1), sub(GRP_W, 2), sub(DH, 0), sub(DH, 1), sub(DH, 2),
                  row(r_mat.shape[1]),
                  _const_spec(wa.shape), _const_spec(wb.shape), _const_spec(wo.shape),
                  _const_spec(gn_g.shape), _const_spec(gn_b.shape)] + [_const_spec(t.shape) for t in tables],
        out_specs=row(D_MODEL),
        out_shape=jax.ShapeDtypeStruct((T, D_MODEL), F32),
        scratch_shapes=[pltpu.VMEM((tm, GRP_W), BF16), pltpu.VMEM((tm, RET_V_W), BF16),
                        pltpu.VMEM((HPG, tm, DH), F32), pltpu.VMEM((HPG, tm, DH), F32),
                        pltpu.VMEM((tm, DH), F32), pltpu.VMEM((tm, DH), F32),
                        pltpu.VMEM((RET_HEADS, RET_DK, RET_DV), F32)],
        compiler_params=_cparams(("parallel", "arbitrary")),
        name="mix",
    )(x2, *o_att, *st_att, r_mat, wa, wb, wo, gn_g, gn_b, *tables)


def _mlp_kernel(x_ref, g2_ref, wu_ref, wd_ref, out_ref, *, tm, rc, ff_chunk):
    for kc in range(tm // rc):
        rows = slice(kc * rc, (kc + 1) * rc)
        x1 = x_ref[rows, :]
        xn2 = _rms_rows(x1, g2_ref[...]).astype(BF16)
        acc = x1
        for c in range(D_FF // ff_chunk):
            cs = slice(c * ff_chunk, (c + 1) * ff_chunk)
            hcol = jnp.maximum(jnp.dot(xn2, wu_ref[:, cs], preferred_element_type=F32), 0.0)
            acc = acc + jnp.dot((hcol * hcol).astype(BF16), wd_ref[cs, :], preferred_element_type=F32)
        out_ref[rows, :] = acc


def _mlp(x1, g2, wu, wd, tm, rc, ff_chunk):
    T = x1.shape[0]
    return pl.pallas_call(
        functools.partial(_mlp_kernel, tm=tm, rc=rc, ff_chunk=ff_chunk),
        grid=(T // tm,),
        in_specs=[pl.BlockSpec((tm, D_MODEL), lambda i: (i, 0)), _const_spec((1, D_MODEL)),
                  _const_spec(wu.shape), _const_spec(wd.shape)],
        out_specs=pl.BlockSpec((tm, D_MODEL), lambda i: (i, 0)),
        out_shape=jax.ShapeDtypeStruct((T, D_MODEL), F32),
        compiler_params=_cparams(("parallel",)),
        name="mlp",
    )(x1, g2, wu, wd)


def kernel(x, norm1_g, w_in, q_norm_g, k_norm_g, ret_gn_g, ret_gn_b, w_proj_a, w_proj_b, w_out, norm2_g, w_up, w_down):
    B, S, D = x.shape
    T = B * S
    depth = w_in.shape[0]
    x2 = x.reshape(T, D)
    for l in range(depth):
        w_bf = w_in[l].astype(BF16)
        g1 = norm1_g[l].reshape(1, D)
        qg = q_norm_g[l] * (DH ** -0.5 * LOG2E)
        kg = k_norm_g[l]

        o_att, st_att = [], []
        for gi, (_, dil) in enumerate(ATT_GROUPS):
            hs = slice(gi * HPG, (gi + 1) * HPG)
            hg = jnp.concatenate([qg[hs].reshape(1, GRP_W), kg[hs].reshape(1, GRP_W)], axis=1)
            o, st = _att_branch(x2, g1, w_bf, hg, gi, B, S, tm=2048, rc=256)
            o_att.append(o)
            st_att.append(st)

        r_mat = _proj_rest(x2, g1, w_bf, tm=512, rc=RET_CHUNK, tn=RET_QK_W)

        x1 = _mix(x2, o_att, st_att, r_mat,
                  w_proj_a[l].astype(BF16), w_proj_b[l].astype(BF16), w_out[l].astype(BF16),
                  ret_gn_g[l].reshape(1, RET_V_W), ret_gn_b[l].reshape(1, RET_V_W), B, S, tm=512, rc=RET_CHUNK)
        x2 = _mlp(x1, norm2_g[l].reshape(1, D), w_up[l].astype(BF16), w_down[l].astype(BF16),
                  tm=1024, rc=256, ff_chunk=1024)
    return x2.reshape(B, S, D)
```

```python
import functools

import numpy as np
import jax
import jax.numpy as jnp
from jax import lax
from jax.experimental import pallas as pl
from jax.experimental.pallas import tpu as pltpu

F32 = jnp.float32
BF16 = jnp.bfloat16

D_MODEL = 1024
ATT_GROUPS = ((128, 1), (512, 4), (2048, 16))
HPG = 4
ATT_HEADS = 12
DH = 128
LANE = 128
NSLAB = D_MODEL // LANE
ATT_BLOCK = 128
ATT_W = ATT_HEADS * DH
GRP_W = HPG * DH
STAT_W = LANE // HPG
RET_HEADS = 4
RET_DK = 256
RET_DV = 512
RET_QK_W = 1024
RET_V_W = 2048
RET_CHUNK = 256
D_FF = 4096
EPS = 1e-6
NEG = -1e30
LOG2E = float(np.log2(np.e))

VMEM_LIMIT = 56 * 1024 * 1024


def _cparams(sem):
    return pltpu.CompilerParams(dimension_semantics=sem, vmem_limit_bytes=VMEM_LIMIT)


def _const_spec(shape, index=None):
    nd = len(shape)
    index = (0,) * nd if index is None else index
    return pl.BlockSpec(shape, lambda *_: index, pipeline_mode=pl.Buffered(1))


def _nt_dot(a, b):
    return lax.dot_general(a, b, (((1,), (1,)), ((), ())), preferred_element_type=F32)


def _tn_dot(a, b):
    return lax.dot_general(a, b, (((0,), (0,)), ((), ())), preferred_element_type=F32)


def _rms_rows(x, g):
    return x * lax.rsqrt(jnp.mean(x * x, axis=-1, keepdims=True) + EPS) * g


def _sigmoid(t):
    return 1.0 / (1.0 + jnp.exp(-t))


def _num_x_refs(dil):
    return 1 if dil == 1 else NSLAB


def _att_kernel(*refs, dil, tm, rc):
    nx = _num_x_refs(dil)
    x_refs = refs[:nx]
    (g1_ref, wq_ref, wk_ref, wv_ref, hg_ref, bias_ref, bias0_ref,
     o_ref, st_ref, qkv_ref, carry_ref) = refs[nx:nx + 11]
    n = tm // dil
    first = pl.program_id(1) == 0

    @pl.when(first)
    def _():
        carry_ref[...] = jnp.zeros_like(carry_ref)

    if dil == 16:
        xa_ref, = refs[nx + 11:]
        n4 = tm // 4
        for c in range(NSLAB):
            for lo in range(4):
                xa_ref[c, lo * n4:(lo + 1) * n4, :] = x_refs[c][pl.ds(lo, n4, stride=4), :]

    def load_rows(c, r, l0, ln):
        if dil == 1:
            return x_refs[0][l0:l0 + ln, c * LANE:(c + 1) * LANE]
        if dil == 16:
            return xa_ref[c, pl.ds((r % 4) * n4 + r // 4 + 4 * l0, ln, stride=4), :]
        return x_refs[c][pl.ds(r + l0 * dil, ln, stride=dil), :]

    def project(kc):
        pieces, p = [], kc * rc
        while p < (kc + 1) * rc:
            r, l0 = divmod(p, n)
            ln = min(n - l0, (kc + 1) * rc - p)
            xs = [load_rows(c, r, l0, ln) for c in range(NSLAB)]
            ss = xs[0] * xs[0]
            for c in range(1, NSLAB):
                ss = ss + xs[c] * xs[c]
            scale = lax.rsqrt(jnp.sum(ss, axis=-1, keepdims=True) * (1.0 / D_MODEL) + EPS)
            pieces.append(jnp.concatenate(
                [(xs[c] * scale * g1_ref[:, c * LANE:(c + 1) * LANE]).astype(BF16) for c in range(NSLAB)], axis=1))
            p += ln
        xn = pieces[0] if len(pieces) == 1 else jnp.concatenate(pieces, axis=0)
        rows = slice(kc * rc, (kc + 1) * rc)
        for part, w_ref in enumerate((wq_ref, wk_ref)):
            acc = jnp.dot(xn, w_ref[...], preferred_element_type=F32)
            for h in range(HPG):
                hs = slice(h * DH, (h + 1) * DH)
                t = acc[:, hs]
                rr = lax.rsqrt(jnp.sum(t * t, axis=-1, keepdims=True) * (1.0 / DH) + EPS)
                gs = slice(part * GRP_W + h * DH, part * GRP_W + (h + 1) * DH)
                qkv_ref[rows, gs] = (t * rr * hg_ref[:, gs]).astype(BF16)
        qkv_ref[rows, 2 * GRP_W:] = jnp.dot(xn, wv_ref[...], preferred_element_type=F32).astype(BF16)

    lane = lax.broadcasted_iota(jnp.int32, (ATT_BLOCK, DH), 1)
    ones = jnp.ones((2 * ATT_BLOCK, DH), BF16)

    def attend(p0):
        r, l0 = divmod(p0, n)
        qrows = slice(p0, p0 + ATT_BLOCK)
        st_blk = None
        for h in range(HPG):
            hs = slice(h * DH, (h + 1) * DH)
            ks = slice(GRP_W + h * DH, GRP_W + (h + 1) * DH)
            vs = slice(2 * GRP_W + h * DH, 2 * GRP_W + (h + 1) * DH)
            q = qkv_ref[qrows, hs]
            if l0 == 0:
                k = jnp.concatenate([carry_ref[r, :, hs], qkv_ref[qrows, ks]], axis=0)
                v = jnp.concatenate([carry_ref[r, :, ks], qkv_ref[qrows, vs]], axis=0)
                b = jnp.where(first, bias0_ref[h], bias_ref[h])
            else:
                krows = slice(p0 - ATT_BLOCK, p0 + ATT_BLOCK)
                k = qkv_ref[krows, ks]
                v = qkv_ref[krows, vs]
                b = bias_ref[h]
            s = _nt_dot(q, k) + b
            m = jnp.max(s, axis=-1, keepdims=True)
            p = jnp.exp2(s - m).astype(BF16)
            oe = jnp.dot(p, jnp.concatenate([v, ones], axis=1), preferred_element_type=F32)
            o_ref[0, r, l0:l0 + ATT_BLOCK, hs] = oe[:, :DH].astype(BF16)
            lo = h * STAT_W
            mb = jnp.broadcast_to(m, (ATT_BLOCK, DH))
            st_blk = mb if h == 0 else jnp.where(lane >= lo, mb, st_blk)
            st_blk = jnp.where(lane >= lo + STAT_W // 2, oe[:, DH:], st_blk)
        st_ref[0, r, l0:l0 + ATT_BLOCK, :] = st_blk

    for kc in range(tm // rc):
        project(kc)
    for t in range(tm // ATT_BLOCK):
        attend(t * ATT_BLOCK)

    for r in range(dil):
        carry_ref[r] = qkv_ref[(r + 1) * n - ATT_BLOCK:(r + 1) * n, GRP_W:]


def _att_branch(x2, g1, w_bf, hg, gi, batch, seq, tm, rc):
    window, dil = ATT_GROUPS[gi]
    nt = seq // tm
    n = tm // dil

    slopes = 2.0 ** (-8.0 * np.arange(1, ATT_HEADS + 1, dtype=np.float32) / ATT_HEADS)
    slopes = slopes[gi * HPG:(gi + 1) * HPG]
    qi = np.arange(ATT_BLOCK)[:, None]
    kj = np.arange(2 * ATT_BLOCK)[None, :]
    dist = ATT_BLOCK + qi - kj
    valid = (dist >= 0) & (dist <= window // dil)
    alibi = -slopes[:, None, None] * (dist * dil).astype(np.float32)[None] * np.float32(LOG2E)
    bias = np.where(valid[None], alibi, NEG).astype(np.float32)
    bias0 = np.where((kj >= ATT_BLOCK)[None], bias, NEG).astype(np.float32)

    nx = _num_x_refs(dil)
    slab = [pl.BlockSpec((tm, D_MODEL // nx), functools.partial(lambda b, i, c: (b * nt + i, c), c=c))
            for c in range(nx)]
    ncb = ATT_W // GRP_W
    w_specs = [_const_spec((D_MODEL, GRP_W), (0, part * ncb + gi)) for part in range(3)]
    gather = [pltpu.VMEM((NSLAB, tm, LANE), F32)] if dil == 16 else []
    return pl.pallas_call(
        functools.partial(_att_kernel, dil=dil, tm=tm, rc=rc),
        grid=(batch, nt),
        in_specs=slab + [_const_spec((1, D_MODEL))] + w_specs
        + [_const_spec(hg.shape), _const_spec(bias.shape), _const_spec(bias0.shape)],
        out_specs=[pl.BlockSpec((1, dil, n, GRP_W), lambda b, i: (b, 0, i, 0)),
                   pl.BlockSpec((1, dil, n, DH), lambda b, i: (b, 0, i, 0))],
        out_shape=[jax.ShapeDtypeStruct((batch, dil, seq // dil, GRP_W), BF16),
                   jax.ShapeDtypeStruct((batch, dil, seq // dil, DH), F32)],
        scratch_shapes=[pltpu.VMEM((tm, 3 * GRP_W), BF16), pltpu.VMEM((dil, ATT_BLOCK, 2 * GRP_W), BF16)] + gather,
        compiler_params=_cparams(("parallel", "arbitrary")),
        name=f"att_d{dil}",
    )(*([x2] * nx), g1, w_bf, w_bf, w_bf, hg, jnp.asarray(bias), jnp.asarray(bias0))


def _proj_rest_kernel(x_ref, g1_ref, w_ref, tq_ref, tk_ref, o_ref, *, tm, rc, tn):
    assert tn == RET_QK_W
    silu_lo = 2 * RET_QK_W + RET_V_W
    sig_lo = silu_lo + RET_V_W
    for kc in range(tm // rc):
        rows = slice(kc * rc, (kc + 1) * rc)
        xn = _rms_rows(x_ref[rows, :], g1_ref[...]).astype(BF16)
        for ct in range(o_ref.shape[1] // tn):
            cs = slice(ct * tn, (ct + 1) * tn)
            acc = jnp.dot(xn, w_ref[:, cs], preferred_element_type=F32)
            if ct == 0:
                acc = acc * tq_ref[...]
            elif ct == 1:
                acc = acc * tk_ref[...]
            elif ct * tn >= sig_lo:
                acc = _sigmoid(acc)
            elif ct * tn >= silu_lo:
                acc = acc * _sigmoid(acc)
            o_ref[rows, cs] = acc.astype(BF16)


def _proj_rest(x2, g1, w_rest, tm, rc, tn):
    T = x2.shape[0]
    n = w_rest.shape[1]
    tabq, tabk = _retention_row_tables(rc)
    return pl.pallas_call(
        functools.partial(_proj_rest_kernel, tm=tm, rc=rc, tn=tn),
        grid=(T // tm,),
        in_specs=[
            pl.BlockSpec((tm, D_MODEL), lambda i: (i, 0)),
            _const_spec((1, D_MODEL)),
            _const_spec(w_rest.shape),
            _const_spec(tabq.shape),
            _const_spec(tabk.shape),
        ],
        out_specs=pl.BlockSpec((tm, n), lambda i: (i, 0)),
        out_shape=jax.ShapeDtypeStruct((T, n), BF16),
        compiler_params=_cparams(("parallel",)),
        name="proj_rest",
    )(x2, g1, w_rest, tabq, tabk)


def _retention_log_gamma():
    return np.log(1.0 - 2.0 ** (-5.0 - np.arange(RET_HEADS, dtype=np.float64)))


def _retention_row_tables(chunk):
    pos = np.arange(chunk, dtype=np.float64)[:, None, None] + 1.0
    lg = _retention_log_gamma()[None, :, None]
    shape = (chunk, RET_HEADS, RET_DK)
    tq = np.broadcast_to(np.exp(lg * pos), shape).reshape(chunk, RET_QK_W)
    tk = np.broadcast_to(np.exp(-lg * pos) * RET_DK ** -0.5, shape).reshape(chunk, RET_QK_W)
    return jnp.asarray(tq, F32), jnp.asarray(tk, F32)


def _retention_chunk_tables(chunk):
    idx = np.arange(chunk)
    mask = (idx[:, None] >= idx[None, :]).astype(np.float32)
    gch = np.exp(_retention_log_gamma() * chunk)
    return jnp.asarray(mask), jnp.asarray(np.broadcast_to(gch[:, None, None], (RET_HEADS, 1, RET_DV)), F32)


def _retention_chunk(qx, kx, v, sg, state_ref, h, gng, gnb, mask_ref, gch_ref):
    s = _nt_dot(qx, kx) * mask_ref[...]
    st = state_ref[h]
    o = (jnp.dot(s.astype(BF16), v, preferred_element_type=F32)
         + jnp.dot(qx, st.astype(BF16), preferred_element_type=F32))
    state_ref[h] = (st + _tn_dot(kx, v)) * gch_ref[h]
    mu = jnp.mean(o, axis=-1, keepdims=True)
    oc = o - mu
    var = jnp.mean(oc * oc, axis=-1, keepdims=True)
    return (oc * lax.rsqrt(var + EPS) * gng + gnb) * sg.astype(F32)


def _mix_kernel(x_ref, o1_ref, o2_ref, o3_ref, l1_ref, l2_ref, l3_ref, r_ref,
                wa_ref, wb_ref, wo_ref, gng_ref, gnb_ref, mask_ref, gch_ref, out_ref,
                oa_ref, or_ref, so2_ref, so3_ref, sl2_ref, sl3_ref, state_ref, *, tm, rc):
    @pl.when(pl.program_id(1) == 0)
    def _():
        state_ref[...] = jnp.zeros_like(state_ref)

    v_lo = 2 * RET_QK_W
    sg_lo = v_lo + RET_V_W
    ga_lo = sg_lo + RET_V_W
    gb_lo = ga_lo + D_MODEL
    for kc in range(tm // rc):
        rows = slice(kc * rc, (kc + 1) * rc)
        for (o_ref, l_ref, so_ref, sl_ref, dil) in ((o2_ref, l2_ref, so2_ref, sl2_ref, ATT_GROUPS[1][1]),
                                                    (o3_ref, l3_ref, so3_ref, sl3_ref, ATT_GROUPS[2][1])):
            n = rc // dil
            src = slice(kc * n, (kc + 1) * n)
            for r in range(dil):
                blk = o_ref[0, r, src, :].astype(F32)
                for h in range(HPG):
                    so_ref[h, pl.ds(kc * rc + r, n, stride=dil), :] = blk[:, h * DH:(h + 1) * DH]
                sl_ref[pl.ds(kc * rc + r, n, stride=dil), :] = l_ref[0, r, src, :]

        st = (l1_ref[0, 0, rows, :], sl2_ref[rows, :], sl3_ref[rows, :])
        mx = jnp.maximum(jnp.maximum(st[0], st[1]), st[2])
        wt = [jnp.exp2(t - mx) for t in st]
        for h in range(HPG):
            hs = slice(h * DH, (h + 1) * DH)
            cm = h * STAT_W
            cd = cm + STAT_W // 2
            wcol = [w[:, cm:cm + 1] for w in wt]
            den = wcol[0] * st[0][:, cd:cd + 1] + wcol[1] * st[1][:, cd:cd + 1] + wcol[2] * st[2][:, cd:cd + 1]
            inv = 1.0 / den
            oa = ((wcol[0] * inv) * o1_ref[0, 0, rows, hs].astype(F32)
                  + (wcol[1] * inv) * so2_ref[h, rows, :]
                  + (wcol[2] * inv) * so3_ref[h, rows, :])
            oa_ref[rows, hs] = oa.astype(BF16)

        for h in range(RET_HEADS):
            qs = slice(h * RET_DK, (h + 1) * RET_DK)
            ks = slice(RET_QK_W + h * RET_DK, RET_QK_W + (h + 1) * RET_DK)
            vs = slice(v_lo + h * RET_DV, v_lo + (h + 1) * RET_DV)
            gs = slice(sg_lo + h * RET_DV, sg_lo + (h + 1) * RET_DV)
            ns = slice(h * RET_DV, (h + 1) * RET_DV)
            y_h = _retention_chunk(r_ref[rows, qs], r_ref[rows, ks], r_ref[rows, vs], r_ref[rows, gs], state_ref, h,
                                   gng_ref[:, ns], gnb_ref[:, ns], mask_ref, gch_ref)
            or_ref[rows, ns] = y_h.astype(BF16)

        ya = jnp.dot(oa_ref[rows, :], wa_ref[...], preferred_element_type=F32)
        yb = jnp.dot(or_ref[rows, :], wb_ref[...], preferred_element_type=F32)
        ga = r_ref[rows, ga_lo:gb_lo].astype(F32)
        gb = r_ref[rows, gb_lo:].astype(F32)
        y = (ga * ya + gb * yb).astype(BF16)
        out_ref[rows, :] = x_ref[rows, :] + jnp.dot(y, wo_ref[...], preferred_element_type=F32)


def _mix(x2, o_att, st_att, r_mat, wa, wb, wo, gn_g, gn_b, batch, seq, tm, rc):
    T = x2.shape[0]
    nt = seq // tm
    tables = _retention_chunk_tables(rc)

    def row(w):
        return pl.BlockSpec((tm, w), lambda b, i: (b * nt + i, 0))

    def sub(w, gi):
        dil = ATT_GROUPS[gi][1]
        return pl.BlockSpec((1, dil, tm // dil, w), lambda b, i: (b, 0, i, 0))

    return pl.pallas_call(
        functools.partial(_mix_kernel, tm=tm, rc=rc),
        grid=(batch, nt),
        in_specs=[row(D_MODEL), sub(GRP_W, 0), sub(GRP_W, 1), sub(GRP_W, 2), sub(DH, 0), sub(DH, 1), sub(DH, 2),
                  row(r_mat.shape[1]),
                  _const_spec(wa.shape), _const_spec(wb.shape), _const_spec(wo.shape),
                  _const_spec(gn_g.shape), _const_spec(gn_b.shape)] + [_const_spec(t.shape) for t in tables],
        out_specs=row(D_MODEL),
        out_shape=jax.ShapeDtypeStruct((T, D_MODEL), F32),
        scratch_shapes=[pltpu.VMEM((tm, GRP_W), BF16), pltpu.VMEM((tm, RET_V_W), BF16),
                        pltpu.VMEM((HPG, tm, DH), F32), pltpu.VMEM((HPG, tm, DH), F32),
                        pltpu.VMEM((tm, DH), F32), pltpu.VMEM((tm, DH), F32),
                        pltpu.VMEM((RET_HEADS, RET_DK, RET_DV), F32)],
        compiler_params=_cparams(("parallel", "arbitrary")),
        name="mix",
    )(x2, *o_att, *st_att, r_mat, wa, wb, wo, gn_g, gn_b, *tables)


def _mlp_kernel(x_ref, g2_ref, wu_ref, wd_ref, out_ref, *, tm, rc, ff_chunk):
    for kc in range(tm // rc):
        rows = slice(kc * rc, (kc + 1) * rc)
        x1 = x_ref[rows, :]
        xn2 = _rms_rows(x1, g2_ref[...]).astype(BF16)
        acc = x1
        for c in range(D_FF // ff_chunk):
            cs = slice(c * ff_chunk, (c + 1) * ff_chunk)
            hcol = jnp.maximum(jnp.dot(xn2, wu_ref[:, cs], preferred_element_type=F32), 0.0)
            acc = acc + jnp.dot((hcol * hcol).astype(BF16), wd_ref[cs, :], preferred_element_type=F32)
        out_ref[rows, :] = acc


def _mlp(x1, g2, wu, wd, tm, rc, ff_chunk):
    T = x1.shape[0]
    return pl.pallas_call(
        functools.partial(_mlp_kernel, tm=tm, rc=rc, ff_chunk=ff_chunk),
        grid=(T // tm,),
        in_specs=[pl.BlockSpec((tm, D_MODEL), lambda i: (i, 0)), _const_spec((1, D_MODEL)),
                  _const_spec(wu.shape), _const_spec(wd.shape)],
        out_specs=pl.BlockSpec((tm, D_MODEL), lambda i: (i, 0)),
        out_shape=jax.ShapeDtypeStruct((T, D_MODEL), F32),
        compiler_params=_cparams(("parallel",)),
        name="mlp",
    )(x1, g2, wu, wd)


def kernel(x, norm1_g, w_in, q_norm_g, k_norm_g, ret_gn_g, ret_gn_b, w_proj_a, w_proj_b, w_out, norm2_g, w_up, w_down):
    B, S, D = x.shape
    T = B * S
    depth = w_in.shape[0]
    x2 = x.reshape(T, D)
    for l in range(depth):
        w_att = w_in[l][:, :3 * ATT_W].astype(BF16)
        w_rest = w_in[l][:, 3 * ATT_W:].astype(BF16)
        g1 = norm1_g[l].reshape(1, D)
        qg = q_norm_g[l] * (DH ** -0.5 * LOG2E)
        kg = k_norm_g[l]

        o_att, st_att = [], []
        for gi, (_, dil) in enumerate(ATT_GROUPS):
            hs = slice(gi * HPG, (gi + 1) * HPG)
            hg = jnp.concatenate([qg[hs].reshape(1, GRP_W), kg[hs].reshape(1, GRP_W)], axis=1)
            o, st = _att_branch(x2, g1, w_att, hg, gi, B, S, tm=2048, rc=1024)
            o_att.append(o)
            st_att.append(st)

        r_mat = _proj_rest(x2, g1, w_rest, tm=512, rc=RET_CHUNK, tn=RET_QK_W)

        x1 = _mix(x2, o_att, st_att, r_mat,
                  w_proj_a[l].astype(BF16), w_proj_b[l].astype(BF16), w_out[l].astype(BF16),
                  ret_gn_g[l].reshape(1, RET_V_W), ret_gn_b[l].reshape(1, RET_V_W), B, S, tm=512, rc=RET_CHUNK)
        x2 = _mlp(x1, norm2_g[l].reshape(1, D), w_up[l].astype(BF16), w_down[l].astype(BF16),
                  tm=1024, rc=256, ff_chunk=1024)
    return x2.reshape(B, S, D)
```

```python
import functools

import numpy as np
import jax
import jax.numpy as jnp
from jax import lax
from jax.experimental import pallas as pl
from jax.experimental.pallas import tpu as pltpu

F32 = jnp.float32
BF16 = jnp.bfloat16

D_MODEL = 1024
ATT_GROUPS = ((128, 1), (512, 4), (2048, 16))
HPG = 4
ATT_HEADS = 12
DH = 128
LANE = 128
NSLAB = D_MODEL // LANE
ATT_BLOCK = 128
ATT_W = ATT_HEADS * DH
GRP_W = HPG * DH
STAT_W = LANE // HPG
RET_HEADS = 4
RET_DK = 256
RET_DV = 512
RET_QK_W = 1024
RET_V_W = 2048
RET_CHUNK = 256
D_FF = 4096
EPS = 1e-6
NEG = -1e30
LOG2E = float(np.log2(np.e))

VMEM_LIMIT = 56 * 1024 * 1024


def _cparams(sem):
    return pltpu.CompilerParams(dimension_semantics=sem, vmem_limit_bytes=VMEM_LIMIT)


def _const_spec(shape, index=None):
    nd = len(shape)
    index = (0,) * nd if index is None else index
    return pl.BlockSpec(shape, lambda *_: index, pipeline_mode=pl.Buffered(1))


def _nt_dot(a, b):
    return lax.dot_general(a, b, (((1,), (1,)), ((), ())), preferred_element_type=F32)


def _tn_dot(a, b):
    return lax.dot_general(a, b, (((0,), (0,)), ((), ())), preferred_element_type=F32)


def _rms_rows(x, g):
    return x * lax.rsqrt(jnp.mean(x * x, axis=-1, keepdims=True) + EPS) * g


def _sigmoid(t):
    return 1.0 / (1.0 + jnp.exp(-t))


def _num_x_refs(dil):
    return 1 if dil == 1 else NSLAB


def _att_kernel(*refs, dil, tm, rc):
    nx = _num_x_refs(dil)
    x_refs = refs[:nx]
    (g1_ref, wq_ref, wk_ref, wv_ref, hg_ref, bias_ref, bias0_ref,
     o_ref, st_ref, qkv_ref, carry_ref) = refs[nx:nx + 11]
    n = tm // dil
    first = pl.program_id(1) == 0

    @pl.when(first)
    def _():
        carry_ref[...] = jnp.zeros_like(carry_ref)

    if dil == 16:
        xa_ref, = refs[nx + 11:]
        n4 = tm // 4
        for c in range(NSLAB):
            for lo in range(4):
                xa_ref[c, lo * n4:(lo + 1) * n4, :] = x_refs[c][pl.ds(lo, n4, stride=4), :]

    def load_rows(c, r, l0, ln):
        if dil == 1:
            return x_refs[0][l0:l0 + ln, c * LANE:(c + 1) * LANE]
        if dil == 16:
            return xa_ref[c, pl.ds((r % 4) * n4 + r // 4 + 4 * l0, ln, stride=4), :]
        return x_refs[c][pl.ds(r + l0 * dil, ln, stride=dil), :]

    def project(kc):
        pieces, p = [], kc * rc
        while p < (kc + 1) * rc:
            r, l0 = divmod(p, n)
            ln = min(n - l0, (kc + 1) * rc - p)
            xs = [load_rows(c, r, l0, ln) for c in range(NSLAB)]
            ss = xs[0] * xs[0]
            for c in range(1, NSLAB):
                ss = ss + xs[c] * xs[c]
            scale = lax.rsqrt(jnp.sum(ss, axis=-1, keepdims=True) * (1.0 / D_MODEL) + EPS)
            pieces.append(jnp.concatenate(
                [(xs[c] * scale * g1_ref[:, c * LANE:(c + 1) * LANE]).astype(BF16) for c in range(NSLAB)], axis=1))
            p += ln
        xn = pieces[0] if len(pieces) == 1 else jnp.concatenate(pieces, axis=0)
        rows = slice(kc * rc, (kc + 1) * rc)
        for part, w_ref in enumerate((wq_ref, wk_ref)):
            acc = jnp.dot(xn, w_ref[...], preferred_element_type=F32)
            for h in range(HPG):
                hs = slice(h * DH, (h + 1) * DH)
                t = acc[:, hs]
                rr = lax.rsqrt(jnp.sum(t * t, axis=-1, keepdims=True) * (1.0 / DH) + EPS)
                gs = slice(part * GRP_W + h * DH, part * GRP_W + (h + 1) * DH)
                qkv_ref[rows, gs] = (t * rr * hg_ref[:, gs]).astype(BF16)
        qkv_ref[rows, 2 * GRP_W:] = jnp.dot(xn, wv_ref[...], preferred_element_type=F32).astype(BF16)

    lane = lax.broadcasted_iota(jnp.int32, (ATT_BLOCK, DH), 1)
    ones = jnp.ones((2 * ATT_BLOCK, DH), BF16)

    def attend(p0):
        r, l0 = divmod(p0, n)
        qrows = slice(p0, p0 + ATT_BLOCK)
        st_blk = None
        for h in range(HPG):
            hs = slice(h * DH, (h + 1) * DH)
            ks = slice(GRP_W + h * DH, GRP_W + (h + 1) * DH)
            vs = slice(2 * GRP_W + h * DH, 2 * GRP_W + (h + 1) * DH)
            q = qkv_ref[qrows, hs]
            if l0 == 0:
                k = jnp.concatenate([carry_ref[r, :, hs], qkv_ref[qrows, ks]], axis=0)
                v = jnp.concatenate([carry_ref[r, :, ks], qkv_ref[qrows, vs]], axis=0)
                b = jnp.where(first, bias0_ref[h], bias_ref[h])
            else:
                krows = slice(p0 - ATT_BLOCK, p0 + ATT_BLOCK)
                k = qkv_ref[krows, ks]
                v = qkv_ref[krows, vs]
                b = bias_ref[h]
            s = _nt_dot(q, k) + b
            m = jnp.max(s, axis=-1, keepdims=True)
            p = jnp.exp2(s - m).astype(BF16)
            oe = jnp.dot(p, jnp.concatenate([v, ones], axis=1), preferred_element_type=F32)
            o_ref[0, r, l0:l0 + ATT_BLOCK, hs] = oe[:, :DH].astype(BF16)
            lo = h * STAT_W
            mb = jnp.broadcast_to(m, (ATT_BLOCK, DH))
            st_blk = mb if h == 0 else jnp.where(lane >= lo, mb, st_blk)
            st_blk = jnp.where(lane >= lo + STAT_W // 2, oe[:, DH:], st_blk)
        st_ref[0, r, l0:l0 + ATT_BLOCK, :] = st_blk

    for kc in range(tm // rc):
        project(kc)
    for t in range(tm // ATT_BLOCK):
        attend(t * ATT_BLOCK)

    for r in range(dil):
        carry_ref[r] = qkv_ref[(r + 1) * n - ATT_BLOCK:(r + 1) * n, GRP_W:]


def _att_branch(x2, g1, w_bf, hg, gi, batch, seq, tm, rc):
    window, dil = ATT_GROUPS[gi]
    nt = seq // tm
    n = tm // dil

    slopes = 2.0 ** (-8.0 * np.arange(1, ATT_HEADS + 1, dtype=np.float32) / ATT_HEADS)
    slopes = slopes[gi * HPG:(gi + 1) * HPG]
    qi = np.arange(ATT_BLOCK)[:, None]
    kj = np.arange(2 * ATT_BLOCK)[None, :]
    dist = ATT_BLOCK + qi - kj
    valid = (dist >= 0) & (dist <= window // dil)
    alibi = -slopes[:, None, None] * (dist * dil).astype(np.float32)[None] * np.float32(LOG2E)
    bias = np.where(valid[None], alibi, NEG).astype(np.float32)
    bias0 = np.where((kj >= ATT_BLOCK)[None], bias, NEG).astype(np.float32)

    nx = _num_x_refs(dil)
    slab = [pl.BlockSpec((tm, D_MODEL // nx), functools.partial(lambda b, i, c: (b * nt + i, c), c=c))
            for c in range(nx)]
    ncb = ATT_W // GRP_W
    w_specs = [_const_spec((D_MODEL, GRP_W), (0, part * ncb + gi)) for part in range(3)]
    gather = [pltpu.VMEM((NSLAB, tm, LANE), F32)] if dil == 16 else []
    return pl.pallas_call(
        functools.partial(_att_kernel, dil=dil, tm=tm, rc=rc),
        grid=(batch, nt),
        in_specs=slab + [_const_spec((1, D_MODEL))] + w_specs
        + [_const_spec(hg.shape), _const_spec(bias.shape), _const_spec(bias0.shape)],
        out_specs=[pl.BlockSpec((1, dil, n, GRP_W), lambda b, i: (b, 0, i, 0)),
                   pl.BlockSpec((1, dil, n, DH), lambda b, i: (b, 0, i, 0))],
        out_shape=[jax.ShapeDtypeStruct((batch, dil, seq // dil, GRP_W), BF16),
                   jax.ShapeDtypeStruct((batch, dil, seq // dil, DH), F32)],
        scratch_shapes=[pltpu.VMEM((tm, 3 * GRP_W), BF16), pltpu.VMEM((dil, ATT_BLOCK, 2 * GRP_W), BF16)] + gather,
        compiler_params=_cparams(("parallel", "arbitrary")),
        name=f"att_d{dil}",
    )(*([x2] * nx), g1, w_bf, w_bf, w_bf, hg, jnp.asarray(bias), jnp.asarray(bias0))


def _proj_rest_kernel(x_ref, g1_ref, w_ref, tq_ref, tk_ref, o_ref, *, tm, rc, tn):
    assert tn == RET_QK_W
    silu_lo = 2 * RET_QK_W + RET_V_W
    sig_lo = silu_lo + RET_V_W
    for kc in range(tm // rc):
        rows = slice(kc * rc, (kc + 1) * rc)
        xn = _rms_rows(x_ref[rows, :], g1_ref[...]).astype(BF16)
        for ct in reversed(range(o_ref.shape[1] // tn)):
            cs = slice(ct * tn, (ct + 1) * tn)
            acc = jnp.dot(xn, w_ref[:, 3 * ATT_W + ct * tn:3 * ATT_W + (ct + 1) * tn], preferred_element_type=F32)
            if ct == 0:
                acc = acc * tq_ref[...]
            elif ct == 1:
                acc = acc * tk_ref[...]
            elif ct * tn >= sig_lo:
                acc = _sigmoid(acc)
            elif ct * tn >= silu_lo:
                acc = acc * _sigmoid(acc)
            o_ref[rows, cs] = acc.astype(BF16)


def _proj_rest(x2, g1, w_bf, tm, rc, tn):
    T = x2.shape[0]
    n = w_bf.shape[1] - 3 * ATT_W
    tabq, tabk = _retention_row_tables(rc, RET_CHUNK)
    return pl.pallas_call(
        functools.partial(_proj_rest_kernel, tm=tm, rc=rc, tn=tn),
        grid=(T // tm,),
        in_specs=[
            pl.BlockSpec((tm, D_MODEL), lambda i: (i, 0)),
            _const_spec((1, D_MODEL)),
            _const_spec(w_bf.shape),
            _const_spec(tabq.shape),
            _const_spec(tabk.shape),
        ],
        out_specs=pl.BlockSpec((tm, n), lambda i: (i, 0)),
        out_shape=jax.ShapeDtypeStruct((T, n), BF16),
        compiler_params=_cparams(("parallel",)),
        name="proj_rest",
    )(x2, g1, w_bf, tabq, tabk)


def _retention_log_gamma():
    return np.log(1.0 - 2.0 ** (-5.0 - np.arange(RET_HEADS, dtype=np.float64)))


def _retention_row_tables(rows, chunk):
    pos = (np.arange(rows) % chunk).astype(np.float64)[:, None, None] + 1.0
    lg = _retention_log_gamma()[None, :, None]
    shape = (rows, RET_HEADS, RET_DK)
    tq = np.broadcast_to(np.exp(lg * pos), shape).reshape(rows, RET_QK_W)
    tk = np.broadcast_to(np.exp(-lg * pos) * RET_DK ** -0.5, shape).reshape(rows, RET_QK_W)
    return jnp.asarray(tq, F32), jnp.asarray(tk, F32)


def _retention_chunk_tables(chunk):
    idx = np.arange(chunk)
    mask = (idx[:, None] >= idx[None, :]).astype(np.float32)
    gch = np.exp(_retention_log_gamma() * chunk)
    return jnp.asarray(mask), jnp.asarray(np.broadcast_to(gch[:, None, None], (RET_HEADS, 1, RET_DV)), F32)


def _retention_chunk(qx, kx, v, sg, state_ref, h, gng, gnb, mask_ref, gch_ref):
    s = _nt_dot(qx, kx) * mask_ref[...]
    st = state_ref[h]
    o = (jnp.dot(s.astype(BF16), v, preferred_element_type=F32)
         + jnp.dot(qx, st.astype(BF16), preferred_element_type=F32))
    state_ref[h] = (st + _tn_dot(kx, v)) * gch_ref[h]
    mu = jnp.mean(o, axis=-1, keepdims=True)
    oc = o - mu
    var = jnp.mean(oc * oc, axis=-1, keepdims=True)
    return (oc * lax.rsqrt(var + EPS) * gng + gnb) * sg.astype(F32)


def _mix_kernel(x_ref, o1_ref, o2_ref, o3_ref, l1_ref, l2_ref, l3_ref, r_ref,
                wa_ref, wb_ref, wo_ref, gng_ref, gnb_ref, mask_ref, gch_ref, out_ref,
                oa_ref, or_ref, so2_ref, so3_ref, sl2_ref, sl3_ref, state_ref, *, tm, rc):
    @pl.when(pl.program_id(1) == 0)
    def _():
        state_ref[...] = jnp.zeros_like(state_ref)

    v_lo = 2 * RET_QK_W
    sg_lo = v_lo + RET_V_W
    ga_lo = sg_lo + RET_V_W
    gb_lo = ga_lo + D_MODEL
    for kc in range(tm // rc):
        rows = slice(kc * rc, (kc + 1) * rc)
        for (o_ref, l_ref, so_ref, sl_ref, dil) in ((o2_ref, l2_ref, so2_ref, sl2_ref, ATT_GROUPS[1][1]),
                                                    (o3_ref, l3_ref, so3_ref, sl3_ref, ATT_GROUPS[2][1])):
            n = rc // dil
            src = slice(kc * n, (kc + 1) * n)
            for r in range(dil):
                blk = o_ref[0, r, src, :].astype(F32)
                for h in range(HPG):
                    so_ref[h, pl.ds(kc * rc + r, n, stride=dil), :] = blk[:, h * DH:(h + 1) * DH]
                sl_ref[pl.ds(kc * rc + r, n, stride=dil), :] = l_ref[0, r, src, :]

        st = (l1_ref[0, 0, rows, :], sl2_ref[rows, :], sl3_ref[rows, :])
        mx = jnp.maximum(jnp.maximum(st[0], st[1]), st[2])
        wt = [jnp.exp2(t - mx) for t in st]
        sums = [pltpu.roll(t, LANE - STAT_W // 2, axis=1) for t in st]
        inv = 1.0 / (wt[0] * sums[0] + wt[1] * sums[1] + wt[2] * sums[2])
        cf = [w * inv for w in wt]
        for h in range(HPG):
            hs = slice(h * DH, (h + 1) * DH)
            cm = h * STAT_W
            oa = (cf[0][:, cm:cm + 1] * o1_ref[0, 0, rows, hs].astype(F32)
                  + cf[1][:, cm:cm + 1] * so2_ref[h, rows, :]
                  + cf[2][:, cm:cm + 1] * so3_ref[h, rows, :])
            oa_ref[rows, hs] = oa.astype(BF16)

        for h in range(RET_HEADS):
            qs = slice(h * RET_DK, (h + 1) * RET_DK)
            ks = slice(RET_QK_W + h * RET_DK, RET_QK_W + (h + 1) * RET_DK)
            vs = slice(v_lo + h * RET_DV, v_lo + (h + 1) * RET_DV)
            gs = slice(sg_lo + h * RET_DV, sg_lo + (h + 1) * RET_DV)
            ns = slice(h * RET_DV, (h + 1) * RET_DV)
            y_h = _retention_chunk(r_ref[rows, qs], r_ref[rows, ks], r_ref[rows, vs], r_ref[rows, gs], state_ref, h,
                                   gng_ref[:, ns], gnb_ref[:, ns], mask_ref, gch_ref)
            or_ref[rows, ns] = y_h.astype(BF16)

        ya = jnp.dot(oa_ref[rows, :], wa_ref[...], preferred_element_type=F32)
        yb = jnp.dot(or_ref[rows, :], wb_ref[...], preferred_element_type=F32)
        ga = r_ref[rows, ga_lo:gb_lo].astype(F32)
        gb = r_ref[rows, gb_lo:].astype(F32)
        y = (ga * ya + gb * yb).astype(BF16)
        out_ref[rows, :] = x_ref[rows, :] + jnp.dot(y, wo_ref[...], preferred_element_type=F32)


def _mix(x2, o_att, st_att, r_mat, wa, wb, wo, gn_g, gn_b, batch, seq, tm, rc):
    T = x2.shape[0]
    nt = seq // tm
    tables = _retention_chunk_tables(rc)

    def row(w):
        return pl.BlockSpec((tm, w), lambda b, i: (b * nt + i, 0))

    def sub(w, gi):
        dil = ATT_GROUPS[gi][1]
        return pl.BlockSpec((1, dil, tm // dil, w), lambda b, i: (b, 0, i, 0))

    return pl.pallas_call(
        functools.partial(_mix_kernel, tm=tm, rc=rc),
        grid=(batch, nt),
        in_specs=[row(D_MODEL), sub(GRP_W, 0), sub(GRP_W, 1), sub(GRP_W, 2), sub(DH, 0), sub(DH, 1), sub(DH, 2),
                  row(r_mat.shape[1]),
                  _const_spec(wa.shape), _const_spec(wb.shape), _const_spec(wo.shape),
                  _const_spec(gn_g.shape), _const_spec(gn_b.shape)] + [_const_spec(t.shape) for t in tables],
        out_specs=row(D_MODEL),
        out_shape=jax.ShapeDtypeStruct((T, D_MODEL), F32),
        scratch_shapes=[pltpu.VMEM((tm, GRP_W), BF16), pltpu.VMEM((tm, RET_V_W), BF16),
                        pltpu.VMEM((HPG, tm, DH), F32), pltpu.VMEM((HPG, tm, DH), F32),
                        pltpu.VMEM((tm, DH), F32), pltpu.VMEM((tm, DH), F32),
                        pltpu.VMEM((RET_HEADS, RET_DK, RET_DV), F32)],
        compiler_params=_cparams(("parallel", "arbitrary")),
        name="mix",
    )(x2, *o_att, *st_att, r_mat, wa, wb, wo, gn_g, gn_b, *tables)


def _mlp_kernel(x_ref, g2_ref, wu_ref, wd_ref, out_ref, *, tm, rc, ff_chunk):
    for kc in range(tm // rc):
        rows = slice(kc * rc, (kc + 1) * rc)
        x1 = x_ref[rows, :]
        xn2 = _rms_rows(x1, g2_ref[...]).astype(BF16)
        acc = x1
        for c in range(D_FF // ff_chunk):
            cs = slice(c * ff_chunk, (c + 1) * ff_chunk)
            hcol = jnp.maximum(jnp.dot(xn2, wu_ref[:, cs], preferred_element_type=F32), 0.0)
            acc = acc + jnp.dot((hcol * hcol).astype(BF16), wd_ref[cs, :], preferred_element_type=F32)
        out_ref[rows, :] = acc


def _mlp(x1, g2, wu, wd, tm, rc, ff_chunk):
    T = x1.shape[0]
    return pl.pallas_call(
        functools.partial(_mlp_kernel, tm=tm, rc=rc, ff_chunk=ff_chunk),
        grid=(T // tm,),
        in_specs=[pl.BlockSpec((tm, D_MODEL), lambda i: (i, 0)), _const_spec((1, D_MODEL)),
                  _const_spec(wu.shape), _const_spec(wd.shape)],
        out_specs=pl.BlockSpec((tm, D_MODEL), lambda i: (i, 0)),
        out_shape=jax.ShapeDtypeStruct((T, D_MODEL), F32),
        compiler_params=_cparams(("parallel",)),
        name="mlp",
    )(x1, g2, wu, wd)


def kernel(x, norm1_g, w_in, q_norm_g, k_norm_g, ret_gn_g, ret_gn_b, w_proj_a, w_proj_b, w_out, norm2_g, w_up, w_down):
    B, S, D = x.shape
    T = B * S
    depth = w_in.shape[0]
    x2 = x.reshape(T, D)
    for l in range(depth):
        w_bf = w_in[l].astype(BF16)
        g1 = norm1_g[l].reshape(1, D)
        qg = q_norm_g[l] * (DH ** -0.5 * LOG2E)
        kg = k_norm_g[l]

        o_att, st_att = [], []
        for gi, (_, dil) in enumerate(ATT_GROUPS):
            hs = slice(gi * HPG, (gi + 1) * HPG)
            hg = jnp.concatenate([qg[hs].reshape(1, GRP_W), kg[hs].reshape(1, GRP_W)], axis=1)
            o, st = _att_branch(x2, g1, w_bf, hg, gi, B, S, tm=2048, rc=1024)
            o_att.append(o)
            st_att.append(st)

        r_mat = _proj_rest(x2, g1, w_bf, tm=512, rc=RET_CHUNK, tn=RET_QK_W)

        x1 = _mix(x2, o_att, st_att, r_mat,
                  w_proj_a[l].astype(BF16), w_proj_b[l].astype(BF16), w_out[l].astype(BF16),
                  ret_gn_g[l].reshape(1, RET_V_W), ret_gn_b[l].reshape(1, RET_V_W), B, S, tm=512, rc=RET_CHUNK)
        x2 = _mlp(x1, norm2_g[l].reshape(1, D), w_up[l].astype(BF16), w_down[l].astype(BF16),
                  tm=1024, rc=256, ff_chunk=1024)
    return x2.reshape(B, S, D)
```

```python
import functools

import numpy as np
import jax
import jax.numpy as jnp
from jax import lax
from jax.experimental import pallas as pl
from jax.experimental.pallas import tpu as pltpu

F32 = jnp.float32
BF16 = jnp.bfloat16

D_MODEL = 1024
ATT_GROUPS = ((128, 1), (512, 4), (2048, 16))
HPG = 4
ATT_HEADS = 12
DH = 128
LANE = 128
NSLAB = D_MODEL // LANE
ATT_BLOCK = 128
ATT_W = ATT_HEADS * DH
GRP_W = HPG * DH
STAT_W = LANE // HPG
RET_HEADS = 4
RET_DK = 256
RET_DV = 512
RET_QK_W = 1024
RET_V_W = 2048
RET_CHUNK = 256
D_FF = 4096
EPS = 1e-6
NEG = -1e30
LOG2E = float(np.log2(np.e))

VMEM_LIMIT = 56 * 1024 * 1024


def _cparams(sem):
    return pltpu.CompilerParams(dimension_semantics=sem, vmem_limit_bytes=VMEM_LIMIT)


def _const_spec(shape, index=None):
    nd = len(shape)
    index = (0,) * nd if index is None else index
    return pl.BlockSpec(shape, lambda *_: index, pipeline_mode=pl.Buffered(1))


def _nt_dot(a, b):
    return lax.dot_general(a, b, (((1,), (1,)), ((), ())), preferred_element_type=F32)


def _tn_dot(a, b):
    return lax.dot_general(a, b, (((0,), (0,)), ((), ())), preferred_element_type=F32)


def _rms_rows(x, g):
    return x * lax.rsqrt(jnp.mean(x * x, axis=-1, keepdims=True) + EPS) * g


def _sigmoid(t):
    return 1.0 / (1.0 + jnp.exp(-t))


def _num_x_refs(dil):
    return 1 if dil == 1 else NSLAB


def _att_kernel(*refs, dil, tm, rc):
    nx = _num_x_refs(dil)
    x_refs = refs[:nx]
    (g1_ref, wq_ref, wk_ref, wv_ref, hg_ref, bias_ref, bias0_ref,
     o_ref, st_ref, qkv_ref, carry_ref) = refs[nx:nx + 11]
    n = tm // dil
    first = pl.program_id(1) == 0

    @pl.when(first)
    def _():
        carry_ref[...] = jnp.zeros_like(carry_ref)

    if dil == 16:
        xa_ref, = refs[nx + 11:]
        n4 = tm // 4
        for c in range(NSLAB):
            for lo in range(4):
                xa_ref[c, lo * n4:(lo + 1) * n4, :] = x_refs[c][pl.ds(lo, n4, stride=4), :]

    def load_rows(c, r, l0, ln):
        if dil == 1:
            return x_refs[0][l0:l0 + ln, c * LANE:(c + 1) * LANE]
        if dil == 16:
            return xa_ref[c, pl.ds((r % 4) * n4 + r // 4 + 4 * l0, ln, stride=4), :]
        return x_refs[c][pl.ds(r + l0 * dil, ln, stride=dil), :]

    def project(kc):
        pieces, p = [], kc * rc
        while p < (kc + 1) * rc:
            r, l0 = divmod(p, n)
            ln = min(n - l0, (kc + 1) * rc - p)
            xs = [load_rows(c, r, l0, ln) for c in range(NSLAB)]
            ss = xs[0] * xs[0]
            for c in range(1, NSLAB):
                ss = ss + xs[c] * xs[c]
            scale = lax.rsqrt(jnp.sum(ss, axis=-1, keepdims=True) * (1.0 / D_MODEL) + EPS)
            pieces.append(jnp.concatenate(
                [(xs[c] * scale * g1_ref[:, c * LANE:(c + 1) * LANE]).astype(BF16) for c in range(NSLAB)], axis=1))
            p += ln
        xn = pieces[0] if len(pieces) == 1 else jnp.concatenate(pieces, axis=0)
        rows = slice(kc * rc, (kc + 1) * rc)
        for part, w_ref in enumerate((wq_ref, wk_ref)):
            acc = jnp.dot(xn, w_ref[...], preferred_element_type=F32)
            for h in range(HPG):
                hs = slice(h * DH, (h + 1) * DH)
                t = acc[:, hs]
                rr = lax.rsqrt(jnp.sum(t * t, axis=-1, keepdims=True) * (1.0 / DH) + EPS)
                gs = slice(part * GRP_W + h * DH, part * GRP_W + (h + 1) * DH)
                qkv_ref[rows, gs] = (t * rr * hg_ref[:, gs]).astype(BF16)
        qkv_ref[rows, 2 * GRP_W:] = jnp.dot(xn, wv_ref[...], preferred_element_type=F32).astype(BF16)

    lane = lax.broadcasted_iota(jnp.int32, (ATT_BLOCK, DH), 1)
    ones = jnp.ones((2 * ATT_BLOCK, DH), BF16)

    def attend(p0):
        r, l0 = divmod(p0, n)
        qrows = slice(p0, p0 + ATT_BLOCK)
        st_blk = None
        for h in range(HPG):
            hs = slice(h * DH, (h + 1) * DH)
            ks = slice(GRP_W + h * DH, GRP_W + (h + 1) * DH)
            vs = slice(2 * GRP_W + h * DH, 2 * GRP_W + (h + 1) * DH)
            q = qkv_ref[qrows, hs]
            if l0 == 0:
                k = jnp.concatenate([carry_ref[r, :, hs], qkv_ref[qrows, ks]], axis=0)
                v = jnp.concatenate([carry_ref[r, :, ks], qkv_ref[qrows, vs]], axis=0)
                b = jnp.where(first, bias0_ref[h], bias_ref[h])
            else:
                krows = slice(p0 - ATT_BLOCK, p0 + ATT_BLOCK)
                k = qkv_ref[krows, ks]
                v = qkv_ref[krows, vs]
                b = bias_ref[h]
            s = _nt_dot(q, k) + b
            m = jnp.max(s, axis=-1, keepdims=True)
            p = jnp.exp2(s - m).astype(BF16)
            oe = jnp.dot(p, jnp.concatenate([v, ones], axis=1), preferred_element_type=F32)
            o_ref[0, r, l0:l0 + ATT_BLOCK, hs] = oe[:, :DH].astype(BF16)
            lo = h * STAT_W
            mb = jnp.broadcast_to(m, (ATT_BLOCK, DH))
            st_blk = mb if h == 0 else jnp.where(lane >= lo, mb, st_blk)
            st_blk = jnp.where(lane >= lo + STAT_W // 2, oe[:, DH:], st_blk)
        st_ref[0, r, l0:l0 + ATT_BLOCK, :] = st_blk

    for kc in range(tm // rc):
        project(kc)
    for t in range(tm // ATT_BLOCK):
        attend(t * ATT_BLOCK)

    for r in range(dil):
        carry_ref[r] = qkv_ref[(r + 1) * n - ATT_BLOCK:(r + 1) * n, GRP_W:]


def _att_branch(x2, g1, w_bf, hg, gi, batch, seq, tm, rc):
    window, dil = ATT_GROUPS[gi]
    nt = seq // tm
    n = tm // dil

    slopes = 2.0 ** (-8.0 * np.arange(1, ATT_HEADS + 1, dtype=np.float32) / ATT_HEADS)
    slopes = slopes[gi * HPG:(gi + 1) * HPG]
    qi = np.arange(ATT_BLOCK)[:, None]
    kj = np.arange(2 * ATT_BLOCK)[None, :]
    dist = ATT_BLOCK + qi - kj
    valid = (dist >= 0) & (dist <= window // dil)
    alibi = -slopes[:, None, None] * (dist * dil).astype(np.float32)[None] * np.float32(LOG2E)
    bias = np.where(valid[None], alibi, NEG).astype(np.float32)
    bias0 = np.where((kj >= ATT_BLOCK)[None], bias, NEG).astype(np.float32)

    nx = _num_x_refs(dil)
    slab = [pl.BlockSpec((tm, D_MODEL // nx), functools.partial(lambda b, i, c: (b * nt + i, c), c=c))
            for c in range(nx)]
    ncb = ATT_W // GRP_W
    w_specs = [_const_spec((D_MODEL, GRP_W), (0, part * ncb + gi)) for part in range(3)]
    gather = [pltpu.VMEM((NSLAB, tm, LANE), F32)] if dil == 16 else []
    return pl.pallas_call(
        functools.partial(_att_kernel, dil=dil, tm=tm, rc=rc),
        grid=(batch, nt),
        in_specs=slab + [_const_spec((1, D_MODEL))] + w_specs
        + [_const_spec(hg.shape), _const_spec(bias.shape), _const_spec(bias0.shape)],
        out_specs=[pl.BlockSpec((1, dil, n, GRP_W), lambda b, i: (b, 0, i, 0)),
                   pl.BlockSpec((1, dil, n, DH), lambda b, i: (b, 0, i, 0))],
        out_shape=[jax.ShapeDtypeStruct((batch, dil, seq // dil, GRP_W), BF16),
                   jax.ShapeDtypeStruct((batch, dil, seq // dil, DH), F32)],
        scratch_shapes=[pltpu.VMEM((tm, 3 * GRP_W), BF16), pltpu.VMEM((dil, ATT_BLOCK, 2 * GRP_W), BF16)] + gather,
        compiler_params=_cparams(("parallel", "arbitrary")),
        name=f"att_d{dil}",
    )(*([x2] * nx), g1, w_bf, w_bf, w_bf, hg, jnp.asarray(bias), jnp.asarray(bias0))


def _merge_groups(kc, rc, o_refs, st_refs, so_refs, sl_refs, oa_ref):
    rows = slice(kc * rc, (kc + 1) * rc)
    for gi in (1, 2):
        dil = ATT_GROUPS[gi][1]
        n = rc // dil
        src = slice(kc * n, (kc + 1) * n)
        for r in range(dil):
            blk = o_refs[gi][0, r, src, :].astype(F32)
            for h in range(HPG):
                so_refs[gi - 1][h, pl.ds(kc * rc + r, n, stride=dil), :] = blk[:, h * DH:(h + 1) * DH]
            sl_refs[gi - 1][pl.ds(kc * rc + r, n, stride=dil), :] = st_refs[gi][0, r, src, :]
    st = (st_refs[0][0, 0, rows, :], sl_refs[0][rows, :], sl_refs[1][rows, :])
    mx = jnp.maximum(jnp.maximum(st[0], st[1]), st[2])
    wt = [jnp.exp2(t - mx) for t in st]
    sums = [pltpu.roll(t, LANE - STAT_W // 2, axis=1) for t in st]
    inv = 1.0 / (wt[0] * sums[0] + wt[1] * sums[1] + wt[2] * sums[2])
    cf = [w * inv for w in wt]
    for h in range(HPG):
        hs = slice(h * DH, (h + 1) * DH)
        cm = h * STAT_W
        oa = (cf[0][:, cm:cm + 1] * o_refs[0][0, 0, rows, hs].astype(F32)
              + cf[1][:, cm:cm + 1] * so_refs[0][h, rows, :]
              + cf[2][:, cm:cm + 1] * so_refs[1][h, rows, :])
        oa_ref[rows, hs] = oa.astype(BF16)


def _proj_rest_kernel(x_ref, g1_ref, w_ref, tq_ref, tk_ref, o_ref, *, tm, rc, tn):
    assert tn == RET_QK_W
    silu_lo = 2 * RET_QK_W + RET_V_W
    sig_lo = silu_lo + RET_V_W
    for kc in range(tm // rc):
        rows = slice(kc * rc, (kc + 1) * rc)
        xn = _rms_rows(x_ref[rows, :], g1_ref[...]).astype(BF16)
        for ct in reversed(range(o_ref.shape[1] // tn)):
            cs = slice(ct * tn, (ct + 1) * tn)
            acc = jnp.dot(xn, w_ref[:, 3 * ATT_W + ct * tn:3 * ATT_W + (ct + 1) * tn], preferred_element_type=F32)
            if ct == 0:
                acc = acc * tq_ref[...]
            elif ct == 1:
                acc = acc * tk_ref[...]
            elif ct * tn >= sig_lo:
                acc = _sigmoid(acc)
            elif ct * tn >= silu_lo:
                acc = acc * _sigmoid(acc)
            o_ref[rows, cs] = acc.astype(BF16)


def _proj_rest(x2, g1, w_bf, tm, rc, tn):
    T = x2.shape[0]
    n = w_bf.shape[1] - 3 * ATT_W
    tabq, tabk = _retention_row_tables(rc, RET_CHUNK)
    return pl.pallas_call(
        functools.partial(_proj_rest_kernel, tm=tm, rc=rc, tn=tn),
        grid=(T // tm,),
        in_specs=[
            pl.BlockSpec((tm, D_MODEL), lambda i: (i, 0)),
            _const_spec((1, D_MODEL)),
            _const_spec(w_bf.shape),
            _const_spec(tabq.shape),
            _const_spec(tabk.shape),
        ],
        out_specs=pl.BlockSpec((tm, n), lambda i: (i, 0)),
        out_shape=jax.ShapeDtypeStruct((T, n), BF16),
        compiler_params=_cparams(("parallel",)),
        name="proj_rest",
    )(x2, g1, w_bf, tabq, tabk)


def _retention_log_gamma():
    return np.log(1.0 - 2.0 ** (-5.0 - np.arange(RET_HEADS, dtype=np.float64)))


def _retention_row_tables(rows, chunk):
    pos = (np.arange(rows) % chunk).astype(np.float64)[:, None, None] + 1.0
    lg = _retention_log_gamma()[None, :, None]
    shape = (rows, RET_HEADS, RET_DK)
    tq = np.broadcast_to(np.exp(lg * pos), shape).reshape(rows, RET_QK_W)
    tk = np.broadcast_to(np.exp(-lg * pos) * RET_DK ** -0.5, shape).reshape(rows, RET_QK_W)
    return jnp.asarray(tq, F32), jnp.asarray(tk, F32)


def _retention_chunk_tables(chunk):
    idx = np.arange(chunk)
    mask = (idx[:, None] >= idx[None, :]).astype(np.float32)
    gch = np.exp(_retention_log_gamma() * chunk)
    return jnp.asarray(mask), jnp.asarray(np.broadcast_to(gch[:, None, None], (RET_HEADS, 1, RET_DV)), F32)


def _retention_chunk(qx, kx, v, sg, state_ref, h, gng, gnb, mask_ref, gch_ref):
    s = _nt_dot(qx, kx) * mask_ref[...]
    st = state_ref[h]
    o = (jnp.dot(s.astype(BF16), v, preferred_element_type=F32)
         + jnp.dot(qx, st.astype(BF16), preferred_element_type=F32))
    state_ref[h] = (st + _tn_dot(kx, v)) * gch_ref[h]
    mu = jnp.mean(o, axis=-1, keepdims=True)
    oc = o - mu
    var = jnp.mean(oc * oc, axis=-1, keepdims=True)
    return (oc * lax.rsqrt(var + EPS) * gng + gnb) * sg.astype(F32)


def _mix_kernel(x_ref, o1_ref, o2_ref, o3_ref, l1_ref, l2_ref, l3_ref, r_ref,
                wa_ref, wb_ref, wo_ref, gng_ref, gnb_ref, mask_ref, gch_ref, out_ref,
                oa_ref, or_ref, so2_ref, so3_ref, sl2_ref, sl3_ref, state_ref, *, tm, rc):
    @pl.when(pl.program_id(1) == 0)
    def _():
        state_ref[...] = jnp.zeros_like(state_ref)

    v_lo = 2 * RET_QK_W
    sg_lo = v_lo + RET_V_W
    ga_lo = sg_lo + RET_V_W
    gb_lo = ga_lo + D_MODEL
    for kc in range(tm // rc):
        rows = slice(kc * rc, (kc + 1) * rc)
        _merge_groups(kc, rc, (o1_ref, o2_ref, o3_ref), (l1_ref, l2_ref, l3_ref),
                      (so2_ref, so3_ref), (sl2_ref, sl3_ref), oa_ref)

        for h in range(RET_HEADS):
            qs = slice(h * RET_DK, (h + 1) * RET_DK)
            ks = slice(RET_QK_W + h * RET_DK, RET_QK_W + (h + 1) * RET_DK)
            vs = slice(v_lo + h * RET_DV, v_lo + (h + 1) * RET_DV)
            gs = slice(sg_lo + h * RET_DV, sg_lo + (h + 1) * RET_DV)
            ns = slice(h * RET_DV, (h + 1) * RET_DV)
            y_h = _retention_chunk(r_ref[rows, qs], r_ref[rows, ks], r_ref[rows, vs], r_ref[rows, gs], state_ref, h,
                                   gng_ref[:, ns], gnb_ref[:, ns], mask_ref, gch_ref)
            or_ref[rows, ns] = y_h.astype(BF16)

        ya = jnp.dot(oa_ref[rows, :], wa_ref[...], preferred_element_type=F32)
        yb = jnp.dot(or_ref[rows, :], wb_ref[...], preferred_element_type=F32)
        ga = r_ref[rows, ga_lo:gb_lo].astype(F32)
        gb = r_ref[rows, gb_lo:].astype(F32)
        y = (ga * ya + gb * yb).astype(BF16)
        out_ref[rows, :] = x_ref[rows, :] + jnp.dot(y, wo_ref[...], preferred_element_type=F32)


def _mix(x2, o_att, st_att, r_mat, wa, wb, wo, gn_g, gn_b, batch, seq, tm, rc):
    T = x2.shape[0]
    nt = seq // tm
    tables = _retention_chunk_tables(rc)

    def row(w):
        return pl.BlockSpec((tm, w), lambda b, i: (b * nt + i, 0))

    def sub(w, gi):
        dil = ATT_GROUPS[gi][1]
        return pl.BlockSpec((1, dil, tm // dil, w), lambda b, i: (b, 0, i, 0))

    return pl.pallas_call(
        functools.partial(_mix_kernel, tm=tm, rc=rc),
        grid=(batch, nt),
        in_specs=[row(D_MODEL), sub(GRP_W, 0), sub(GRP_W, 1), sub(GRP_W, 2), sub(DH, 0), sub(DH, 1), sub(DH, 2),
                  row(r_mat.shape[1]),
                  _const_spec(wa.shape), _const_spec(wb.shape), _const_spec(wo.shape),
                  _const_spec(gn_g.shape), _const_spec(gn_b.shape)] + [_const_spec(t.shape) for t in tables],
        out_specs=row(D_MODEL),
        out_shape=jax.ShapeDtypeStruct((T, D_MODEL), F32),
        scratch_shapes=[pltpu.VMEM((tm, GRP_W), BF16), pltpu.VMEM((tm, RET_V_W), BF16),
                        pltpu.VMEM((HPG, tm, DH), F32), pltpu.VMEM((HPG, tm, DH), F32),
                        pltpu.VMEM((tm, DH), F32), pltpu.VMEM((tm, DH), F32),
                        pltpu.VMEM((RET_HEADS, RET_DK, RET_DV), F32)],
        compiler_params=_cparams(("parallel", "arbitrary")),
        name="mix",
    )(x2, *o_att, *st_att, r_mat, wa, wb, wo, gn_g, gn_b, *tables)


def _mlp_kernel(x_ref, g2_ref, wu_ref, wd_ref, out_ref, *, tm, rc, ff_chunk):
    for kc in range(tm // rc):
        rows = slice(kc * rc, (kc + 1) * rc)
        x1 = x_ref[rows, :]
        xn2 = _rms_rows(x1, g2_ref[...]).astype(BF16)
        acc = x1
        for c in range(D_FF // ff_chunk):
            cs = slice(c * ff_chunk, (c + 1) * ff_chunk)
            hcol = jnp.maximum(jnp.dot(xn2, wu_ref[:, cs], preferred_element_type=F32), 0.0)
            acc = acc + jnp.dot((hcol * hcol).astype(BF16), wd_ref[cs, :], preferred_element_type=F32)
        out_ref[rows, :] = acc


def _mlp(x1, g2, wu, wd, tm, rc, ff_chunk):
    T = x1.shape[0]
    return pl.pallas_call(
        functools.partial(_mlp_kernel, tm=tm, rc=rc, ff_chunk=ff_chunk),
        grid=(T // tm,),
        in_specs=[pl.BlockSpec((tm, D_MODEL), lambda i: (i, 0)), _const_spec((1, D_MODEL)),
                  _const_spec(wu.shape), _const_spec(wd.shape)],
        out_specs=pl.BlockSpec((tm, D_MODEL), lambda i: (i, 0)),
        out_shape=jax.ShapeDtypeStruct((T, D_MODEL), F32),
        compiler_params=_cparams(("parallel",)),
        name="mlp",
    )(x1, g2, wu, wd)


def kernel(x, norm1_g, w_in, q_norm_g, k_norm_g, ret_gn_g, ret_gn_b, w_proj_a, w_proj_b, w_out, norm2_g, w_up, w_down):
    B, S, D = x.shape
    T = B * S
    depth = w_in.shape[0]
    x2 = x.reshape(T, D)
    for l in range(depth):
        w_bf = w_in[l].astype(BF16)
        g1 = norm1_g[l].reshape(1, D)
        qg = q_norm_g[l] * (DH ** -0.5 * LOG2E)
        kg = k_norm_g[l]

        o_att, st_att = [], []
        for gi, (_, dil) in enumerate(ATT_GROUPS):
            hs = slice(gi * HPG, (gi + 1) * HPG)
            hg = jnp.concatenate([qg[hs].reshape(1, GRP_W), kg[hs].reshape(1, GRP_W)], axis=1)
            o, st = _att_branch(x2, g1, w_bf, hg, gi, B, S, tm=2048, rc=1024)
            o_att.append(o)
            st_att.append(st)

        r_mat = _proj_rest(x2, g1, w_bf, tm=512, rc=RET_CHUNK, tn=RET_QK_W)

        x1 = _mix(x2, o_att, st_att, r_mat,
                  w_proj_a[l].astype(BF16), w_proj_b[l].astype(BF16), w_out[l].astype(BF16),
                  ret_gn_g[l].reshape(1, RET_V_W), ret_gn_b[l].reshape(1, RET_V_W), B, S, tm=512, rc=RET_CHUNK)
        x2 = _mlp(x1, norm2_g[l].reshape(1, D), w_up[l].astype(BF16), w_down[l].astype(BF16),
                  tm=1024, rc=256, ff_chunk=1024)
    return x2.reshape(B, S, D)
```

```python
import functools

import numpy as np
import jax
import jax.numpy as jnp
from jax import lax
from jax.experimental import pallas as pl
from jax.experimental.pallas import tpu as pltpu

F32 = jnp.float32
BF16 = jnp.bfloat16

D_MODEL = 1024
ATT_GROUPS = ((128, 1), (512, 4), (2048, 16))
HPG = 4
ATT_HEADS = 12
DH = 128
LANE = 128
NSLAB = D_MODEL // LANE
ATT_BLOCK = 128
ATT_W = ATT_HEADS * DH
GRP_W = HPG * DH
STAT_W = LANE // HPG
RET_HEADS = 4
RET_DK = 256
RET_DV = 512
RET_QK_W = 1024
RET_V_W = 2048
RET_CHUNK = 256
D_FF = 4096
EPS = 1e-6
NEG = -1e30
LOG2E = float(np.log2(np.e))

VMEM_LIMIT = 56 * 1024 * 1024


def _cparams(sem):
    return pltpu.CompilerParams(dimension_semantics=sem, vmem_limit_bytes=VMEM_LIMIT)


def _const_spec(shape, index=None):
    nd = len(shape)
    index = (0,) * nd if index is None else index
    return pl.BlockSpec(shape, lambda *_: index, pipeline_mode=pl.Buffered(1))


def _nt_dot(a, b):
    return lax.dot_general(a, b, (((1,), (1,)), ((), ())), preferred_element_type=F32)


def _tn_dot(a, b):
    return lax.dot_general(a, b, (((0,), (0,)), ((), ())), preferred_element_type=F32)


def _rms_rows(x, g):
    return x * lax.rsqrt(jnp.mean(x * x, axis=-1, keepdims=True) + EPS) * g


def _sigmoid(t):
    return 1.0 / (1.0 + jnp.exp(-t))


def _num_x_refs(dil):
    return 1 if dil == 1 else NSLAB


def _att_kernel(*refs, dil, tm, rc):
    nx = _num_x_refs(dil)
    x_refs = refs[:nx]
    (g1_ref, wq_ref, wk_ref, wv_ref, hg_ref, bias_ref,
     o_ref, st_ref, qkv_ref, carry_ref, bsel_ref) = refs[nx:nx + 11]
    n = tm // dil
    first = pl.program_id(1) == 0

    @pl.when(first)
    def _():
        carry_ref[...] = jnp.zeros_like(carry_ref)

    no_prev = jnp.logical_and(first, lax.broadcasted_iota(jnp.int32, (ATT_BLOCK, 2 * ATT_BLOCK), 1) < ATT_BLOCK)
    for h in range(HPG):
        bsel_ref[h] = jnp.where(no_prev, NEG, bias_ref[h])

    if dil == 16:
        xa_ref, = refs[nx + 11:]
        n4 = tm // 4
        for c in range(NSLAB):
            for lo in range(4):
                xa_ref[c, lo * n4:(lo + 1) * n4, :] = x_refs[c][pl.ds(lo, n4, stride=4), :]

    def load_rows(c, r, l0, ln):
        if dil == 1:
            return x_refs[0][l0:l0 + ln, c * LANE:(c + 1) * LANE]
        if dil == 16:
            return xa_ref[c, pl.ds((r % 4) * n4 + r // 4 + 4 * l0, ln, stride=4), :]
        return x_refs[c][pl.ds(r + l0 * dil, ln, stride=dil), :]

    def project(kc):
        pieces, p = [], kc * rc
        while p < (kc + 1) * rc:
            r, l0 = divmod(p, n)
            ln = min(n - l0, (kc + 1) * rc - p)
            xs = [load_rows(c, r, l0, ln) for c in range(NSLAB)]
            ss = xs[0] * xs[0]
            for c in range(1, NSLAB):
                ss = ss + xs[c] * xs[c]
            scale = lax.rsqrt(jnp.sum(ss, axis=-1, keepdims=True) * (1.0 / D_MODEL) + EPS)
            pieces.append(jnp.concatenate(
                [(xs[c] * scale * g1_ref[:, c * LANE:(c + 1) * LANE]).astype(BF16) for c in range(NSLAB)], axis=1))
            p += ln
        xn = pieces[0] if len(pieces) == 1 else jnp.concatenate(pieces, axis=0)
        rows = slice(kc * rc, (kc + 1) * rc)
        for part, w_ref in enumerate((wq_ref, wk_ref)):
            acc = jnp.dot(xn, w_ref[...], preferred_element_type=F32)
            for h in range(HPG):
                hs = slice(h * DH, (h + 1) * DH)
                t = acc[:, hs]
                rr = lax.rsqrt(jnp.sum(t * t, axis=-1, keepdims=True) * (1.0 / DH) + EPS)
                gs = slice(part * GRP_W + h * DH, part * GRP_W + (h + 1) * DH)
                qkv_ref[rows, gs] = (t * rr * hg_ref[:, gs]).astype(BF16)
        qkv_ref[rows, 2 * GRP_W:] = jnp.dot(xn, wv_ref[...], preferred_element_type=F32).astype(BF16)

    lane = lax.broadcasted_iota(jnp.int32, (ATT_BLOCK, DH), 1)
    ones = jnp.ones((2 * ATT_BLOCK, DH), BF16)

    def attend(p0):
        r, l0 = divmod(p0, n)
        qrows = slice(p0, p0 + ATT_BLOCK)
        st_blk = None
        for h in range(HPG):
            hs = slice(h * DH, (h + 1) * DH)
            ks = slice(GRP_W + h * DH, GRP_W + (h + 1) * DH)
            vs = slice(2 * GRP_W + h * DH, 2 * GRP_W + (h + 1) * DH)
            q = qkv_ref[qrows, hs]
            if l0 == 0:
                k = jnp.concatenate([carry_ref[r, :, hs], qkv_ref[qrows, ks]], axis=0)
                v = jnp.concatenate([carry_ref[r, :, ks], qkv_ref[qrows, vs]], axis=0)
                b = bsel_ref[h]
            else:
                krows = slice(p0 - ATT_BLOCK, p0 + ATT_BLOCK)
                k = qkv_ref[krows, ks]
                v = qkv_ref[krows, vs]
                b = bias_ref[h]
            s = _nt_dot(q, k) + b
            m = jnp.max(s, axis=-1, keepdims=True)
            p = jnp.exp2(s - m).astype(BF16)
            oe = jnp.dot(p, jnp.concatenate([v, ones], axis=1), preferred_element_type=F32)
            o_ref[0, r, l0:l0 + ATT_BLOCK, hs] = oe[:, :DH].astype(BF16)
            lo = h * STAT_W
            mb = jnp.broadcast_to(m, (ATT_BLOCK, DH))
            st_blk = mb if h == 0 else jnp.where(lane >= lo, mb, st_blk)
            st_blk = jnp.where(lane >= lo + STAT_W // 2, oe[:, DH:], st_blk)
        st_ref[0, r, l0:l0 + ATT_BLOCK, :] = st_blk

    for kc in range(tm // rc):
        project(kc)
    for t in range(tm // ATT_BLOCK):
        attend(t * ATT_BLOCK)

    for r in range(dil):
        carry_ref[r] = qkv_ref[(r + 1) * n - ATT_BLOCK:(r + 1) * n, GRP_W:]


def _att_branch(x2, g1, w_bf, hg, gi, batch, seq, tm, rc):
    window, dil = ATT_GROUPS[gi]
    nt = seq // tm
    n = tm // dil

    slopes = 2.0 ** (-8.0 * np.arange(1, ATT_HEADS + 1, dtype=np.float32) / ATT_HEADS)
    slopes = slopes[gi * HPG:(gi + 1) * HPG]
    qi = np.arange(ATT_BLOCK)[:, None]
    kj = np.arange(2 * ATT_BLOCK)[None, :]
    dist = ATT_BLOCK + qi - kj
    valid = (dist >= 0) & (dist <= window // dil)
    alibi = -slopes[:, None, None] * (dist * dil).astype(np.float32)[None] * np.float32(LOG2E)
    bias = np.where(valid[None], alibi, NEG).astype(np.float32)

    nx = _num_x_refs(dil)
    slab = [pl.BlockSpec((tm, D_MODEL // nx), functools.partial(lambda b, i, c: (b * nt + i, c), c=c))
            for c in range(nx)]
    ncb = ATT_W // GRP_W
    w_specs = [_const_spec((D_MODEL, GRP_W), (0, part * ncb + gi)) for part in range(3)]
    gather = [pltpu.VMEM((NSLAB, tm, LANE), F32)] if dil == 16 else []
    return pl.pallas_call(
        functools.partial(_att_kernel, dil=dil, tm=tm, rc=rc),
        grid=(batch, nt),
        in_specs=slab + [_const_spec((1, D_MODEL))] + w_specs
        + [_const_spec(hg.shape), _const_spec(bias.shape)],
        out_specs=[pl.BlockSpec((1, dil, n, GRP_W), lambda b, i: (b, 0, i, 0)),
                   pl.BlockSpec((1, dil, n, DH), lambda b, i: (b, 0, i, 0))],
        out_shape=[jax.ShapeDtypeStruct((batch, dil, seq // dil, GRP_W), BF16),
                   jax.ShapeDtypeStruct((batch, dil, seq // dil, DH), F32)],
        scratch_shapes=[pltpu.VMEM((tm, 3 * GRP_W), BF16), pltpu.VMEM((dil, ATT_BLOCK, 2 * GRP_W), BF16),
                        pltpu.VMEM(bias.shape, F32)] + gather,
        compiler_params=_cparams(("parallel", "arbitrary")),
        name=f"att_d{dil}",
    )(*([x2] * nx), g1, w_bf, w_bf, w_bf, hg, jnp.asarray(bias))


def _merge_groups(kc, rc, o_refs, st_refs, so_refs, sl_refs, oa_ref):
    rows = slice(kc * rc, (kc + 1) * rc)
    for gi in (1, 2):
        dil = ATT_GROUPS[gi][1]
        n = rc // dil
        src = slice(kc * n, (kc + 1) * n)
        for r in range(dil):
            blk = o_refs[gi][0, r, src, :].astype(F32)
            for h in range(HPG):
                so_refs[gi - 1][h, pl.ds(kc * rc + r, n, stride=dil), :] = blk[:, h * DH:(h + 1) * DH]
            sl_refs[gi - 1][pl.ds(kc * rc + r, n, stride=dil), :] = st_refs[gi][0, r, src, :]
    st = (st_refs[0][0, 0, rows, :], sl_refs[0][rows, :], sl_refs[1][rows, :])
    mx = jnp.maximum(jnp.maximum(st[0], st[1]), st[2])
    wt = [jnp.exp2(t - mx) for t in st]
    sums = [pltpu.roll(t, LANE - STAT_W // 2, axis=1) for t in st]
    inv = 1.0 / (wt[0] * sums[0] + wt[1] * sums[1] + wt[2] * sums[2])
    cf = [w * inv for w in wt]
    for h in range(HPG):
        hs = slice(h * DH, (h + 1) * DH)
        cm = h * STAT_W
        oa = (cf[0][:, cm:cm + 1] * o_refs[0][0, 0, rows, hs].astype(F32)
              + cf[1][:, cm:cm + 1] * so_refs[0][h, rows, :]
              + cf[2][:, cm:cm + 1] * so_refs[1][h, rows, :])
        oa_ref[rows, hs] = oa.astype(BF16)


def _proj_rest_kernel(x_ref, g1_ref, w_ref, tq_ref, tk_ref, o_ref, *, tm, rc, tn):
    assert tn == RET_QK_W
    silu_lo = 2 * RET_QK_W + RET_V_W
    sig_lo = silu_lo + RET_V_W
    for kc in range(tm // rc):
        rows = slice(kc * rc, (kc + 1) * rc)
        xn = _rms_rows(x_ref[rows, :], g1_ref[...]).astype(BF16)
        for ct in reversed(range(o_ref.shape[1] // tn)):
            cs = slice(ct * tn, (ct + 1) * tn)
            acc = jnp.dot(xn, w_ref[:, 3 * ATT_W + ct * tn:3 * ATT_W + (ct + 1) * tn], preferred_element_type=F32)
            if ct == 0:
                acc = acc * tq_ref[...]
            elif ct == 1:
                acc = acc * tk_ref[...]
            elif ct * tn >= sig_lo:
                acc = _sigmoid(acc)
            elif ct * tn >= silu_lo:
                acc = acc * _sigmoid(acc)
            o_ref[rows, cs] = acc.astype(BF16)


def _proj_rest(x2, g1, w_bf, tm, rc, tn):
    T = x2.shape[0]
    n = w_bf.shape[1] - 3 * ATT_W
    tabq, tabk = _retention_row_tables(rc, RET_CHUNK)
    return pl.pallas_call(
        functools.partial(_proj_rest_kernel, tm=tm, rc=rc, tn=tn),
        grid=(T // tm,),
        in_specs=[
            pl.BlockSpec((tm, D_MODEL), lambda i: (i, 0)),
            _const_spec((1, D_MODEL)),
            _const_spec(w_bf.shape),
            _const_spec(tabq.shape),
            _const_spec(tabk.shape),
        ],
        out_specs=pl.BlockSpec((tm, n), lambda i: (i, 0)),
        out_shape=jax.ShapeDtypeStruct((T, n), BF16),
        compiler_params=_cparams(("parallel",)),
        name="proj_rest",
    )(x2, g1, w_bf, tabq, tabk)


def _retention_log_gamma():
    return np.log(1.0 - 2.0 ** (-5.0 - np.arange(RET_HEADS, dtype=np.float64)))


def _retention_row_tables(rows, chunk):
    pos = (np.arange(rows) % chunk).astype(np.float64)[:, None, None] + 1.0
    lg = _retention_log_gamma()[None, :, None]
    shape = (rows, RET_HEADS, RET_DK)
    tq = np.broadcast_to(np.exp(lg * pos), shape).reshape(rows, RET_QK_W)
    tk = np.broadcast_to(np.exp(-lg * pos) * RET_DK ** -0.5, shape).reshape(rows, RET_QK_W)
    return jnp.asarray(tq, F32), jnp.asarray(tk, F32)


def _retention_chunk_tables(chunk):
    idx = np.arange(chunk)
    mask = (idx[:, None] >= idx[None, :]).astype(np.float32)
    gch = np.exp(_retention_log_gamma() * chunk)
    return jnp.asarray(mask), jnp.asarray(np.broadcast_to(gch[:, None, None], (RET_HEADS, 1, RET_DV)), F32)


def _retention_chunk(qx, kx, v, sg, state_ref, h, gng, gnb, mask_ref, gch_ref):
    s = _nt_dot(qx, kx) * mask_ref[...]
    st = state_ref[h]
    o = (jnp.dot(s.astype(BF16), v, preferred_element_type=F32)
         + jnp.dot(qx, st.astype(BF16), preferred_element_type=F32))
    state_ref[h] = (st + _tn_dot(kx, v)) * gch_ref[h]
    mu = jnp.mean(o, axis=-1, keepdims=True)
    oc = o - mu
    var = jnp.mean(oc * oc, axis=-1, keepdims=True)
    return (oc * lax.rsqrt(var + EPS) * gng + gnb) * sg.astype(F32)


def _mix_kernel(x_ref, o1_ref, o2_ref, o3_ref, l1_ref, l2_ref, l3_ref, r_ref,
                wa_ref, wb_ref, wo_ref, gng_ref, gnb_ref, mask_ref, gch_ref, out_ref,
                oa_ref, or_ref, so2_ref, so3_ref, sl2_ref, sl3_ref, state_ref, *, tm, rc):
    @pl.when(pl.program_id(1) == 0)
    def _():
        state_ref[...] = jnp.zeros_like(state_ref)

    v_lo = 2 * RET_QK_W
    sg_lo = v_lo + RET_V_W
    ga_lo = sg_lo + RET_V_W
    gb_lo = ga_lo + D_MODEL
    for kc in range(tm // rc):
        rows = slice(kc * rc, (kc + 1) * rc)
        _merge_groups(kc, rc, (o1_ref, o2_ref, o3_ref), (l1_ref, l2_ref, l3_ref),
                      (so2_ref, so3_ref), (sl2_ref, sl3_ref), oa_ref)

        for h in range(RET_HEADS):
            qs = slice(h * RET_DK, (h + 1) * RET_DK)
            ks = slice(RET_QK_W + h * RET_DK, RET_QK_W + (h + 1) * RET_DK)
            vs = slice(v_lo + h * RET_DV, v_lo + (h + 1) * RET_DV)
            gs = slice(sg_lo + h * RET_DV, sg_lo + (h + 1) * RET_DV)
            ns = slice(h * RET_DV, (h + 1) * RET_DV)
            y_h = _retention_chunk(r_ref[rows, qs], r_ref[rows, ks], r_ref[rows, vs], r_ref[rows, gs], state_ref, h,
                                   gng_ref[:, ns], gnb_ref[:, ns], mask_ref, gch_ref)
            or_ref[rows, ns] = y_h.astype(BF16)

        ya = jnp.dot(oa_ref[rows, :], wa_ref[...], preferred_element_type=F32)
        yb = jnp.dot(or_ref[rows, :], wb_ref[...], preferred_element_type=F32)
        ga = r_ref[rows, ga_lo:gb_lo].astype(F32)
        gb = r_ref[rows, gb_lo:].astype(F32)
        y = (ga * ya + gb * yb).astype(BF16)
        out_ref[rows, :] = x_ref[rows, :] + jnp.dot(y, wo_ref[...], preferred_element_type=F32)


def _mix(x2, o_att, st_att, r_mat, wa, wb, wo, gn_g, gn_b, batch, seq, tm, rc):
    T = x2.shape[0]
    nt = seq // tm
    tables = _retention_chunk_tables(rc)

    def row(w):
        return pl.BlockSpec((tm, w), lambda b, i: (b * nt + i, 0))

    def sub(w, gi):
        dil = ATT_GROUPS[gi][1]
        return pl.BlockSpec((1, dil, tm // dil, w), lambda b, i: (b, 0, i, 0))

    return pl.pallas_call(
        functools.partial(_mix_kernel, tm=tm, rc=rc),
        grid=(batch, nt),
        in_specs=[row(D_MODEL), sub(GRP_W, 0), sub(GRP_W, 1), sub(GRP_W, 2), sub(DH, 0), sub(DH, 1), sub(DH, 2),
                  row(r_mat.shape[1]),
                  _const_spec(wa.shape), _const_spec(wb.shape), _const_spec(wo.shape),
                  _const_spec(gn_g.shape), _const_spec(gn_b.shape)] + [_const_spec(t.shape) for t in tables],
        out_specs=row(D_MODEL),
        out_shape=jax.ShapeDtypeStruct((T, D_MODEL), F32),
        scratch_shapes=[pltpu.VMEM((tm, GRP_W), BF16), pltpu.VMEM((tm, RET_V_W), BF16),
                        pltpu.VMEM((HPG, tm, DH), F32), pltpu.VMEM((HPG, tm, DH), F32),
                        pltpu.VMEM((tm, DH), F32), pltpu.VMEM((tm, DH), F32),
                        pltpu.VMEM((RET_HEADS, RET_DK, RET_DV), F32)],
        compiler_params=_cparams(("parallel", "arbitrary")),
        name="mix",
    )(x2, *o_att, *st_att, r_mat, wa, wb, wo, gn_g, gn_b, *tables)


def _mlp_kernel(x_ref, g2_ref, wu_ref, wd_ref, out_ref, *, tm, rc, ff_chunk):
    for kc in range(tm // rc):
        rows = slice(kc * rc, (kc + 1) * rc)
        x1 = x_ref[rows, :]
        xn2 = _rms_rows(x1, g2_ref[...]).astype(BF16)
        acc = x1
        for c in range(D_FF // ff_chunk):
            cs = slice(c * ff_chunk, (c + 1) * ff_chunk)
            hcol = jnp.maximum(jnp.dot(xn2, wu_ref[:, cs], preferred_element_type=F32), 0.0)
            acc = acc + jnp.dot((hcol * hcol).astype(BF16), wd_ref[cs, :], preferred_element_type=F32)
        out_ref[rows, :] = acc


def _mlp(x1, g2, wu, wd, tm, rc, ff_chunk):
    T = x1.shape[0]
    return pl.pallas_call(
        functools.partial(_mlp_kernel, tm=tm, rc=rc, ff_chunk=ff_chunk),
        grid=(T // tm,),
        in_specs=[pl.BlockSpec((tm, D_MODEL), lambda i: (i, 0)), _const_spec((1, D_MODEL)),
                  _const_spec(wu.shape), _const_spec(wd.shape)],
        out_specs=pl.BlockSpec((tm, D_MODEL), lambda i: (i, 0)),
        out_shape=jax.ShapeDtypeStruct((T, D_MODEL), F32),
        compiler_params=_cparams(("parallel",)),
        name="mlp",
    )(x1, g2, wu, wd)


def kernel(x, norm1_g, w_in, q_norm_g, k_norm_g, ret_gn_g, ret_gn_b, w_proj_a, w_proj_b, w_out, norm2_g, w_up, w_down):
    B, S, D = x.shape
    T = B * S
    depth = w_in.shape[0]
    x2 = x.reshape(T, D)
    for l in range(depth):
        w_bf = w_in[l].astype(BF16)
        g1 = norm1_g[l].reshape(1, D)
        qg = q_norm_g[l] * (DH ** -0.5 * LOG2E)
        kg = k_norm_g[l]

        o_att, st_att = [], []
        for gi, (_, dil) in enumerate(ATT_GROUPS):
            hs = slice(gi * HPG, (gi + 1) * HPG)
            hg = jnp.concatenate([qg[hs].reshape(1, GRP_W), kg[hs].reshape(1, GRP_W)], axis=1)
            o, st = _att_branch(x2, g1, w_bf, hg, gi, B, S, tm=2048, rc=1024)
            o_att.append(o)
            st_att.append(st)

        r_mat = _proj_rest(x2, g1, w_bf, tm=512, rc=RET_CHUNK, tn=RET_QK_W)

        x1 = _mix(x2, o_att, st_att, r_mat,
                  w_proj_a[l].astype(BF16), w_proj_b[l].astype(BF16), w_out[l].astype(BF16),
                  ret_gn_g[l].reshape(1, RET_V_W), ret_gn_b[l].reshape(1, RET_V_W), B, S, tm=512, rc=RET_CHUNK)
        x2 = _mlp(x1, norm2_g[l].reshape(1, D), w_up[l].astype(BF16), w_down[l].astype(BF16),
                  tm=1024, rc=256, ff_chunk=1024)
    return x2.reshape(B, S, D)
```

```python
import functools

import numpy as np
import jax
import jax.numpy as jnp
from jax import lax
from jax.experimental import pallas as pl
from jax.experimental.pallas import tpu as pltpu

F32 = jnp.float32
BF16 = jnp.bfloat16

D_MODEL = 1024
ATT_GROUPS = ((128, 1), (512, 4), (2048, 16))
HPG = 4
ATT_HEADS = 12
DH = 128
LANE = 128
NSLAB = D_MODEL // LANE
ATT_BLOCK = 128
ATT_W = ATT_HEADS * DH
GRP_W = HPG * DH
STAT_W = LANE // HPG
RET_HEADS = 4
RET_DK = 256
RET_DV = 512
RET_QK_W = 1024
RET_V_W = 2048
RET_CHUNK = 256
D_FF = 4096
EPS = 1e-6
NEG = -1e30
LOG2E = float(np.log2(np.e))

VMEM_LIMIT = 56 * 1024 * 1024


def _cparams(sem):
    return pltpu.CompilerParams(dimension_semantics=sem, vmem_limit_bytes=VMEM_LIMIT)


def _const_spec(shape, index=None):
    nd = len(shape)
    index = (0,) * nd if index is None else index
    return pl.BlockSpec(shape, lambda *_: index, pipeline_mode=pl.Buffered(1))


def _nt_dot(a, b):
    return lax.dot_general(a, b, (((1,), (1,)), ((), ())), preferred_element_type=F32)


def _tn_dot(a, b):
    return lax.dot_general(a, b, (((0,), (0,)), ((), ())), preferred_element_type=F32)


def _rms_rows(x, g):
    return x * lax.rsqrt(jnp.mean(x * x, axis=-1, keepdims=True) + EPS) * g


def _sigmoid(t):
    return 1.0 / (1.0 + jnp.exp(-t))


def _num_x_refs(dil):
    return 1 if dil == 1 else NSLAB


def _att_kernel(*refs, dil, tm, rc):
    nx = _num_x_refs(dil)
    x_refs = refs[:nx]
    (g1_ref, wq_ref, wk_ref, wv_ref, hg_ref, bias_ref,
     o_ref, st_ref, qkv_ref, carry_ref, bsel_ref) = refs[nx:nx + 11]
    n = tm // dil
    first = pl.program_id(1) == 0

    @pl.when(first)
    def _():
        carry_ref[...] = jnp.zeros_like(carry_ref)

    no_prev = jnp.logical_and(first, lax.broadcasted_iota(jnp.int32, (ATT_BLOCK, 2 * ATT_BLOCK), 1) < ATT_BLOCK)
    for h in range(HPG):
        bsel_ref[h] = jnp.where(no_prev, NEG, bias_ref[h])

    if dil == 16:
        xa_ref, = refs[nx + 11:]
        n4 = tm // 4
        for c in range(NSLAB):
            for lo in range(4):
                xa_ref[c, lo * n4:(lo + 1) * n4, :] = x_refs[c][pl.ds(lo, n4, stride=4), :]

    def load_rows(c, r, l0, ln):
        if dil == 1:
            return x_refs[0][l0:l0 + ln, c * LANE:(c + 1) * LANE]
        if dil == 16:
            return xa_ref[c, pl.ds((r % 4) * n4 + r // 4 + 4 * l0, ln, stride=4), :]
        return x_refs[c][pl.ds(r + l0 * dil, ln, stride=dil), :]

    def project(kc):
        pieces, p = [], kc * rc
        while p < (kc + 1) * rc:
            r, l0 = divmod(p, n)
            ln = min(n - l0, (kc + 1) * rc - p)
            xs = [load_rows(c, r, l0, ln) for c in range(NSLAB)]
            ss = xs[0] * xs[0]
            for c in range(1, NSLAB):
                ss = ss + xs[c] * xs[c]
            scale = lax.rsqrt(jnp.sum(ss, axis=-1, keepdims=True) * (1.0 / D_MODEL) + EPS)
            pieces.append(jnp.concatenate(
                [(xs[c] * scale * g1_ref[:, c * LANE:(c + 1) * LANE]).astype(BF16) for c in range(NSLAB)], axis=1))
            p += ln
        xn = pieces[0] if len(pieces) == 1 else jnp.concatenate(pieces, axis=0)
        rows = slice(kc * rc, (kc + 1) * rc)
        for part, w_ref in enumerate((wq_ref, wk_ref)):
            acc = jnp.dot(xn, w_ref[...], preferred_element_type=F32)
            for h in range(HPG):
                hs = slice(h * DH, (h + 1) * DH)
                t = acc[:, hs]
                rr = lax.rsqrt(jnp.sum(t * t, axis=-1, keepdims=True) * (1.0 / DH) + EPS)
                gs = slice(part * GRP_W + h * DH, part * GRP_W + (h + 1) * DH)
                qkv_ref[rows, gs] = (t * rr * hg_ref[:, gs]).astype(BF16)
        qkv_ref[rows, 2 * GRP_W:] = jnp.dot(xn, wv_ref[...], preferred_element_type=F32).astype(BF16)

    lane = lax.broadcasted_iota(jnp.int32, (ATT_BLOCK, DH), 1)
    ones = jnp.ones((2 * ATT_BLOCK, DH), BF16)

    def attend(p0):
        r, l0 = divmod(p0, n)
        qrows = slice(p0, p0 + ATT_BLOCK)
        st_blk = None
        for h in range(HPG):
            hs = slice(h * DH, (h + 1) * DH)
            ks = slice(GRP_W + h * DH, GRP_W + (h + 1) * DH)
            vs = slice(2 * GRP_W + h * DH, 2 * GRP_W + (h + 1) * DH)
            q = qkv_ref[qrows, hs]
            if l0 == 0:
                k = jnp.concatenate([carry_ref[r, :, hs], qkv_ref[qrows, ks]], axis=0)
                v = jnp.concatenate([carry_ref[r, :, ks], qkv_ref[qrows, vs]], axis=0)
                b = bsel_ref[h]
            else:
                krows = slice(p0 - ATT_BLOCK, p0 + ATT_BLOCK)
                k = qkv_ref[krows, ks]
                v = qkv_ref[krows, vs]
                b = bias_ref[h]
            s = _nt_dot(q, k) + b
            m = jnp.max(s, axis=-1, keepdims=True)
            p = jnp.exp2(s - m).astype(BF16)
            oe = jnp.dot(p, jnp.concatenate([v, ones], axis=1), preferred_element_type=F32)
            o_ref[0, r, l0:l0 + ATT_BLOCK, hs] = oe[:, :DH].astype(BF16)
            lo = h * STAT_W
            mb = jnp.broadcast_to(m, (ATT_BLOCK, DH))
            st_blk = mb if h == 0 else jnp.where(lane >= lo, mb, st_blk)
            st_blk = jnp.where(lane >= lo + STAT_W // 2, oe[:, DH:], st_blk)
        st_ref[0, r, l0:l0 + ATT_BLOCK, :] = st_blk

    for kc in range(tm // rc):
        project(kc)
    for t in range(tm // ATT_BLOCK):
        attend(t * ATT_BLOCK)

    for r in range(dil):
        carry_ref[r] = qkv_ref[(r + 1) * n - ATT_BLOCK:(r + 1) * n, GRP_W:]


def _att_branch(x2, g1, w_bf, hg, gi, batch, seq, tm, rc):
    window, dil = ATT_GROUPS[gi]
    nt = seq // tm
    n = tm // dil

    slopes = 2.0 ** (-8.0 * np.arange(1, ATT_HEADS + 1, dtype=np.float32) / ATT_HEADS)
    slopes = slopes[gi * HPG:(gi + 1) * HPG]
    qi = np.arange(ATT_BLOCK)[:, None]
    kj = np.arange(2 * ATT_BLOCK)[None, :]
    dist = ATT_BLOCK + qi - kj
    valid = (dist >= 0) & (dist <= window // dil)
    alibi = -slopes[:, None, None] * (dist * dil).astype(np.float32)[None] * np.float32(LOG2E)
    bias = np.where(valid[None], alibi, NEG).astype(np.float32)

    nx = _num_x_refs(dil)
    slab = [pl.BlockSpec((tm, D_MODEL // nx), functools.partial(lambda b, i, c: (b * nt + i, c), c=c))
            for c in range(nx)]
    ncb = ATT_W // GRP_W
    w_specs = [_const_spec((D_MODEL, GRP_W), (0, part * ncb + gi)) for part in range(3)]
    gather = [pltpu.VMEM((NSLAB, tm, LANE), F32)] if dil == 16 else []
    return pl.pallas_call(
        functools.partial(_att_kernel, dil=dil, tm=tm, rc=rc),
        grid=(batch, nt),
        in_specs=slab + [_const_spec((1, D_MODEL))] + w_specs
        + [_const_spec(hg.shape), _const_spec(bias.shape)],
        out_specs=[pl.BlockSpec((1, dil, n, GRP_W), lambda b, i: (b, 0, i, 0)),
                   pl.BlockSpec((1, dil, n, DH), lambda b, i: (b, 0, i, 0))],
        out_shape=[jax.ShapeDtypeStruct((batch, dil, seq // dil, GRP_W), BF16),
                   jax.ShapeDtypeStruct((batch, dil, seq // dil, DH), F32)],
        scratch_shapes=[pltpu.VMEM((tm, 3 * GRP_W), BF16), pltpu.VMEM((dil, ATT_BLOCK, 2 * GRP_W), BF16),
                        pltpu.VMEM(bias.shape, F32)] + gather,
        compiler_params=_cparams(("parallel", "arbitrary")),
        name=f"att_d{dil}",
    )(*([x2] * nx), g1, w_bf, w_bf, w_bf, hg, jnp.asarray(bias))


def _merge_groups(kc, rc, o_refs, st_refs, so_refs, sl_refs, oa_ref):
    rows = slice(kc * rc, (kc + 1) * rc)
    for gi in (1, 2):
        dil = ATT_GROUPS[gi][1]
        n = rc // dil
        src = slice(kc * n, (kc + 1) * n)
        for r in range(dil):
            blk = o_refs[gi][0, r, src, :].astype(F32)
            for h in range(HPG):
                so_refs[gi - 1][h, pl.ds(kc * rc + r, n, stride=dil), :] = blk[:, h * DH:(h + 1) * DH]
            sl_refs[gi - 1][pl.ds(kc * rc + r, n, stride=dil), :] = st_refs[gi][0, r, src, :]
    st = (st_refs[0][0, 0, rows, :], sl_refs[0][rows, :], sl_refs[1][rows, :])
    mx = jnp.maximum(jnp.maximum(st[0], st[1]), st[2])
    wt = [jnp.exp2(t - mx) for t in st]
    sums = [pltpu.roll(t, LANE - STAT_W // 2, axis=1) for t in st]
    inv = 1.0 / (wt[0] * sums[0] + wt[1] * sums[1] + wt[2] * sums[2])
    cf = [w * inv for w in wt]
    for h in range(HPG):
        hs = slice(h * DH, (h + 1) * DH)
        cm = h * STAT_W
        oa = (cf[0][:, cm:cm + 1] * o_refs[0][0, 0, rows, hs].astype(F32)
              + cf[1][:, cm:cm + 1] * so_refs[0][h, rows, :]
              + cf[2][:, cm:cm + 1] * so_refs[1][h, rows, :])
        oa_ref[rows, hs] = oa.astype(BF16)


def _proj_rest_kernel(x_ref, g1_ref, w_ref, tq_ref, tk_ref, o_ref, *, tm, rc, tn):
    assert tn == RET_QK_W
    silu_lo = 2 * RET_QK_W + RET_V_W
    sig_lo = silu_lo + RET_V_W
    for kc in range(tm // rc):
        rows = slice(kc * rc, (kc + 1) * rc)
        xn = _rms_rows(x_ref[rows, :], g1_ref[...]).astype(BF16)
        for ct in reversed(range(o_ref.shape[1] // tn)):
            cs = slice(ct * tn, (ct + 1) * tn)
            acc = jnp.dot(xn, w_ref[:, 3 * ATT_W + ct * tn:3 * ATT_W + (ct + 1) * tn], preferred_element_type=F32)
            if ct == 0:
                acc = acc * tq_ref[...]
            elif ct == 1:
                acc = acc * tk_ref[...]
            elif ct * tn >= sig_lo:
                acc = _sigmoid(acc)
            elif ct * tn >= silu_lo:
                acc = acc * _sigmoid(acc)
            o_ref[rows, cs] = acc.astype(BF16)


def _proj_rest(x2, g1, w_bf, tm, rc, tn):
    T = x2.shape[0]
    n = w_bf.shape[1] - 3 * ATT_W
    tabq, tabk = _retention_row_tables(rc, RET_CHUNK)
    return pl.pallas_call(
        functools.partial(_proj_rest_kernel, tm=tm, rc=rc, tn=tn),
        grid=(T // tm,),
        in_specs=[
            pl.BlockSpec((tm, D_MODEL), lambda i: (i, 0)),
            _const_spec((1, D_MODEL)),
            _const_spec(w_bf.shape),
            _const_spec(tabq.shape),
            _const_spec(tabk.shape),
        ],
        out_specs=pl.BlockSpec((tm, n), lambda i: (i, 0)),
        out_shape=jax.ShapeDtypeStruct((T, n), BF16),
        compiler_params=_cparams(("parallel",)),
        name="proj_rest",
    )(x2, g1, w_bf, tabq, tabk)


def _retention_log_gamma():
    return np.log(1.0 - 2.0 ** (-5.0 - np.arange(RET_HEADS, dtype=np.float64)))


def _retention_row_tables(rows, chunk):
    pos = (np.arange(rows) % chunk).astype(np.float64)[:, None, None] + 1.0
    lg = _retention_log_gamma()[None, :, None]
    shape = (rows, RET_HEADS, RET_DK)
    tq = np.broadcast_to(np.exp(lg * pos), shape).reshape(rows, RET_QK_W)
    tk = np.broadcast_to(np.exp(-lg * pos) * RET_DK ** -0.5, shape).reshape(rows, RET_QK_W)
    return jnp.asarray(tq, F32), jnp.asarray(tk, F32)


def _retention_chunk_tables(chunk):
    idx = np.arange(chunk)
    mask = (idx[:, None] >= idx[None, :]).astype(np.float32)
    gch = np.exp(_retention_log_gamma() * chunk)
    return jnp.asarray(mask), jnp.asarray(np.broadcast_to(gch[:, None, None], (RET_HEADS, 1, RET_DV)), F32)


def _retention_chunk(qx, kx, v, sg, state_ref, h, gng, gnb, mask_ref, gch_ref):
    s = _nt_dot(qx, kx) * mask_ref[...]
    sb = s.astype(BF16)
    hw = RET_DV // 2
    os = []
    for c in range(2):
        cs = slice(c * hw, (c + 1) * hw)
        st = state_ref[h, :, cs]
        os.append(jnp.dot(sb, v[:, cs], preferred_element_type=F32)
                  + jnp.dot(qx, st.astype(BF16), preferred_element_type=F32))
        state_ref[h, :, cs] = (st + _tn_dot(kx, v[:, cs])) * gch_ref[h, :, cs]
    mu = (jnp.sum(os[0], axis=-1, keepdims=True) + jnp.sum(os[1], axis=-1, keepdims=True)) * (1.0 / RET_DV)
    ocs = [o - mu for o in os]
    var = (jnp.sum(ocs[0] * ocs[0], axis=-1, keepdims=True)
           + jnp.sum(ocs[1] * ocs[1], axis=-1, keepdims=True)) * (1.0 / RET_DV)
    rstd = lax.rsqrt(var + EPS)
    return jnp.concatenate(
        [(ocs[c] * rstd * gng[:, c * hw:(c + 1) * hw] + gnb[:, c * hw:(c + 1) * hw])
         * sg[:, c * hw:(c + 1) * hw].astype(F32) for c in range(2)], axis=1)


def _mix_kernel(x_ref, o1_ref, o2_ref, o3_ref, l1_ref, l2_ref, l3_ref, r_ref,
                wa_ref, wb_ref, wo_ref, gng_ref, gnb_ref, mask_ref, gch_ref, out_ref,
                oa_ref, or_ref, so2_ref, so3_ref, sl2_ref, sl3_ref, state_ref, *, tm, rc):
    @pl.when(pl.program_id(1) == 0)
    def _():
        state_ref[...] = jnp.zeros_like(state_ref)

    v_lo = 2 * RET_QK_W
    sg_lo = v_lo + RET_V_W
    ga_lo = sg_lo + RET_V_W
    gb_lo = ga_lo + D_MODEL
    for kc in range(tm // rc):
        rows = slice(kc * rc, (kc + 1) * rc)
        _merge_groups(kc, rc, (o1_ref, o2_ref, o3_ref), (l1_ref, l2_ref, l3_ref),
                      (so2_ref, so3_ref), (sl2_ref, sl3_ref), oa_ref)

        for h in range(RET_HEADS):
            qs = slice(h * RET_DK, (h + 1) * RET_DK)
            ks = slice(RET_QK_W + h * RET_DK, RET_QK_W + (h + 1) * RET_DK)
            vs = slice(v_lo + h * RET_DV, v_lo + (h + 1) * RET_DV)
            gs = slice(sg_lo + h * RET_DV, sg_lo + (h + 1) * RET_DV)
            ns = slice(h * RET_DV, (h + 1) * RET_DV)
            y_h = _retention_chunk(r_ref[rows, qs], r_ref[rows, ks], r_ref[rows, vs], r_ref[rows, gs], state_ref, h,
                                   gng_ref[:, ns], gnb_ref[:, ns], mask_ref, gch_ref)
            or_ref[rows, ns] = y_h.astype(BF16)

        ya = jnp.dot(oa_ref[rows, :], wa_ref[...], preferred_element_type=F32)
        yb = jnp.dot(or_ref[rows, :], wb_ref[...], preferred_element_type=F32)
        ga = r_ref[rows, ga_lo:gb_lo].astype(F32)
        gb = r_ref[rows, gb_lo:].astype(F32)
        y = (ga * ya + gb * yb).astype(BF16)
        out_ref[rows, :] = x_ref[rows, :] + jnp.dot(y, wo_ref[...], preferred_element_type=F32)


def _mix(x2, o_att, st_att, r_mat, wa, wb, wo, gn_g, gn_b, batch, seq, tm, rc):
    T = x2.shape[0]
    nt = seq // tm
    tables = _retention_chunk_tables(rc)

    def row(w):
        return pl.BlockSpec((tm, w), lambda b, i: (b * nt + i, 0))

    def sub(w, gi):
        dil = ATT_GROUPS[gi][1]
        return pl.BlockSpec((1, dil, tm // dil, w), lambda b, i: (b, 0, i, 0))

    return pl.pallas_call(
        functools.partial(_mix_kernel, tm=tm, rc=rc),
        grid=(batch, nt),
        in_specs=[row(D_MODEL), sub(GRP_W, 0), sub(GRP_W, 1), sub(GRP_W, 2), sub(DH, 0), sub(DH, 1), sub(DH, 2),
                  row(r_mat.shape[1]),
                  _const_spec(wa.shape), _const_spec(wb.shape), _const_spec(wo.shape),
                  _const_spec(gn_g.shape), _const_spec(gn_b.shape)] + [_const_spec(t.shape) for t in tables],
        out_specs=row(D_MODEL),
        out_shape=jax.ShapeDtypeStruct((T, D_MODEL), F32),
        scratch_shapes=[pltpu.VMEM((tm, GRP_W), BF16), pltpu.VMEM((tm, RET_V_W), BF16),
                        pltpu.VMEM((HPG, tm, DH), F32), pltpu.VMEM((HPG, tm, DH), F32),
                        pltpu.VMEM((tm, DH), F32), pltpu.VMEM((tm, DH), F32),
                        pltpu.VMEM((RET_HEADS, RET_DK, RET_DV), F32)],
        compiler_params=_cparams(("parallel", "arbitrary")),
        name="mix",
    )(x2, *o_att, *st_att, r_mat, wa, wb, wo, gn_g, gn_b, *tables)


def _mlp_kernel(x_ref, g2_ref, wu_ref, wd_ref, out_ref, *, tm, rc, ff_chunk):
    for kc in range(tm // rc):
        rows = slice(kc * rc, (kc + 1) * rc)
        x1 = x_ref[rows, :]
        xn2 = _rms_rows(x1, g2_ref[...]).astype(BF16)
        acc = x1
        for c in range(D_FF // ff_chunk):
            cs = slice(c * ff_chunk, (c + 1) * ff_chunk)
            hcol = jnp.maximum(jnp.dot(xn2, wu_ref[:, cs], preferred_element_type=F32), 0.0)
            acc = acc + jnp.dot((hcol * hcol).astype(BF16), wd_ref[cs, :], preferred_element_type=F32)
        out_ref[rows, :] = acc


def _mlp(x1, g2, wu, wd, tm, rc, ff_chunk):
    T = x1.shape[0]
    return pl.pallas_call(
        functools.partial(_mlp_kernel, tm=tm, rc=rc, ff_chunk=ff_chunk),
        grid=(T // tm,),
        in_specs=[pl.BlockSpec((tm, D_MODEL), lambda i: (i, 0)), _const_spec((1, D_MODEL)),
                  _const_spec(wu.shape), _const_spec(wd.shape)],
        out_specs=pl.BlockSpec((tm, D_MODEL), lambda i: (i, 0)),
        out_shape=jax.ShapeDtypeStruct((T, D_MODEL), F32),
        compiler_params=_cparams(("parallel",)),
        name="mlp",
    )(x1, g2, wu, wd)


def kernel(x, norm1_g, w_in, q_norm_g, k_norm_g, ret_gn_g, ret_gn_b, w_proj_a, w_proj_b, w_out, norm2_g, w_up, w_down):
    B, S, D = x.shape
    T = B * S
    depth = w_in.shape[0]
    x2 = x.reshape(T, D)
    for l in range(depth):
        w_bf = w_in[l].astype(BF16)
        g1 = norm1_g[l].reshape(1, D)
        qg = q_norm_g[l] * (DH ** -0.5 * LOG2E)
        kg = k_norm_g[l]

        o_att, st_att = [], []
        for gi, (_, dil) in enumerate(ATT_GROUPS):
            hs = slice(gi * HPG, (gi + 1) * HPG)
            hg = jnp.concatenate([qg[hs].reshape(1, GRP_W), kg[hs].reshape(1, GRP_W)], axis=1)
            o, st = _att_branch(x2, g1, w_bf, hg, gi, B, S, tm=2048, rc=1024)
            o_att.append(o)
            st_att.append(st)

        r_mat = _proj_rest(x2, g1, w_bf, tm=512, rc=RET_CHUNK, tn=RET_QK_W)

        x1 = _mix(x2, o_att, st_att, r_mat,
                  w_proj_a[l].astype(BF16), w_proj_b[l].astype(BF16), w_out[l].astype(BF16),
                  ret_gn_g[l].reshape(1, RET_V_W), ret_gn_b[l].reshape(1, RET_V_W), B, S, tm=512, rc=RET_CHUNK)
        x2 = _mlp(x1, norm2_g[l].reshape(1, D), w_up[l].astype(BF16), w_down[l].astype(BF16),
                  tm=1024, rc=256, ff_chunk=1024)
    return x2.reshape(B, S, D)
```
